```python
import math
import jax, jax.numpy as jnp
from jax import lax
import numpy as np

D_MODEL = 1024
BATCH = 16
SEQ = 2048
DEPTH = 2

GRID_W = 64
HEAD_DIM = 64
A_HEADS = 8
A_KV_HEADS = 2
B_HEADS = 8
B_KV_HEADS = 2
WINDOW = 128
C_HEADS = 8
C_NOPE_DIM = 64
C_ROPE_DIM = 32
C_V_DIM = 64
C_Q_LORA = 384
C_KV_LORA = 256
Q_BLOCK = 128
N_BUCKETS = 32
MAX_DISTANCE = 128
N_EXPERTS = 16
N_GROUPS = 4
EXPERTS_PER_GROUP = N_EXPERTS // N_GROUPS
TOP_K = 2
D_FF_EXPERT = 512
ROPE_THETA = 10000.0
NORM_EPS = 1e-6
NEG_INF = -1e30
DEEPNORM_ALPHA = (2 * DEPTH) ** 0.25
DEEPNORM_BETA = (8 * DEPTH) ** -0.25
A_Q_W = A_HEADS * HEAD_DIM
A_KV_W = A_KV_HEADS * HEAD_DIM
B_Q_W = B_HEADS * HEAD_DIM
B_KV_W = B_KV_HEADS * HEAD_DIM
IN_SPLITS = (A_Q_W, A_KV_W, A_KV_W, B_Q_W, B_KV_W, B_KV_W,
             C_Q_LORA, C_KV_LORA, C_ROPE_DIM, D_MODEL, D_MODEL, D_MODEL)
IN_COLS = sum(IN_SPLITS)
A_OUT_W = A_HEADS * HEAD_DIM
B_OUT_W = B_HEADS * HEAD_DIM
C_OUT_W = C_HEADS * C_V_DIM

kernel_name = 'hybrid_gated_grid_window_mla_grouped_moe_encoder'


def layer_norm(x, g, b):
    xf = x.astype(jnp.float32)
    mu = jnp.mean(xf, axis=-1, keepdims=True)
    var = jnp.mean(jnp.square(xf - mu), axis=-1, keepdims=True)
    y = (xf - mu) * lax.rsqrt(var + NORM_EPS) * g.astype(jnp.float32) + b.astype(jnp.float32)
    return y.astype(x.dtype)


def rms_norm(x, g):
    xf = x.astype(jnp.float32)
    y = xf * lax.rsqrt(jnp.mean(jnp.square(xf), axis=-1, keepdims=True) + NORM_EPS)
    return (y * g.astype(jnp.float32)).astype(x.dtype)


def rope_cos_sin(pos, dim):
    inv = ROPE_THETA ** (-(jnp.arange(0, dim, 2, dtype=jnp.float32) / dim))
    ang = pos.astype(jnp.float32)[:, None] * inv[None, :]
    return jnp.cos(ang), jnp.sin(ang)


def apply_rope(x, cos, sin):
    half = x.shape[-1] // 2
    x1, x2 = x[..., :half], x[..., half:]
    c = cos[:, None, :].astype(x.dtype)
    s = sin[:, None, :].astype(x.dtype)
    return jnp.concatenate([x1 * c - x2 * s, x2 * c + x1 * s], axis=-1)


def axial_rope(x, row_cs, col_cs):
    half = x.shape[-1] // 2
    return jnp.concatenate([apply_rope(x[..., :half], *row_cs),
                            apply_rope(x[..., half:], *col_cs)], axis=-1)


def t5_bucket(rel):
    nb = N_BUCKETS // 2
    ret = (rel > 0).astype(np.int32) * nb
    n = np.abs(rel)
    max_exact = nb // 2
    large = max_exact + (np.log(np.maximum(n, 1) / max_exact)
                         / math.log(MAX_DISTANCE / max_exact) * (nb - max_exact)).astype(np.int32)
    large = np.minimum(large, nb - 1)
    return (ret + np.where(n < max_exact, n, large)).astype(np.int32)


def band_relative_bias(rpb_table):
    span = Q_BLOCK + 2 * WINDOW
    rel = np.arange(span)[None, :] - WINDOW - np.arange(Q_BLOCK)[:, None]
    bias = rpb_table.astype(jnp.float32)[t5_bucket(rel)]
    bias = jnp.transpose(bias, (2, 0, 1))
    return bias.reshape(B_KV_HEADS, B_HEADS // B_KV_HEADS, Q_BLOCK, span)


def dense_block_attention(q, k, v, scale):
    b, s, kvh, g, dq = q.shape
    nb = s // Q_BLOCK
    qb = q.reshape(b, nb, Q_BLOCK, kvh, g, dq).transpose(1, 0, 2, 3, 4, 5)

    def one(qblk):
        logits = jnp.einsum('bqkgd,bskd->bkgqs', qblk, k,
                            preferred_element_type=jnp.float32) * scale
        p = jax.nn.softmax(logits, axis=-1).astype(v.dtype)
        return jnp.einsum('bkgqs,bskd->bqkgd', p, v)

    out = lax.map(one, qb)
    return out.transpose(1, 0, 2, 3, 4, 5).reshape(b, s, kvh, g, v.shape[-1])


def windowed_sink_attention(q, k, v, sink, bias, scale):
    b, s, kvh, g, d = q.shape
    nb = s // Q_BLOCK
    span = Q_BLOCK + 2 * WINDOW
    pad = ((0, 0), (WINDOW, WINDOW), (0, 0), (0, 0))
    kp = jnp.pad(k, pad)
    vp = jnp.pad(v, pad)
    qb = q.reshape(b, nb, Q_BLOCK, kvh, g, d).transpose(1, 0, 2, 3, 4, 5)
    rel = np.arange(span)[None, :] - WINDOW - np.arange(Q_BLOCK)[:, None]
    band = jnp.asarray(np.abs(rel) <= WINDOW)
    sink_f = sink.astype(jnp.float32)[None, :, :, None, None]

    def one(args):
        i, qblk = args
        start = i * Q_BLOCK
        kb = lax.dynamic_slice_in_dim(kp, start, span, axis=1)
        vb = lax.dynamic_slice_in_dim(vp, start, span, axis=1)
        key_pos = start - WINDOW + jnp.arange(span)
        valid = band & ((key_pos >= 0) & (key_pos < s))[None, :]
        logits = jnp.einsum('bqkgd,bskd->bkgqs', qblk, kb,
                            preferred_element_type=jnp.float32) * scale + bias[None]
        logits = jnp.where(valid, logits, NEG_INF)
        sink_col = jnp.broadcast_to(sink_f, logits.shape[:-1] + (1,))
        p = jax.nn.softmax(jnp.concatenate([logits, sink_col], axis=-1), axis=-1)
        p = p[..., :span].astype(v.dtype)
        return jnp.einsum('bkgqs,bskd->bqkgd', p, vb)

    out = lax.map(one, (jnp.arange(nb), qb))
    return out.transpose(1, 0, 2, 3, 4, 5).reshape(b, s, kvh, g, d)


def token_mixer(h, w_in, a_qn_g, a_kn_g, b_sink, rpb_bias, c_qn_g, c_kvn_g, c_w_uq, c_w_ukv,
                w_br_a, w_br_b, w_br_c, w_o, row_cs, col_cs, seq_cs):
    b, s, _ = h.shape
    proj = h @ w_in
    parts = []
    off = 0
    for width in IN_SPLITS:
        parts.append(proj[..., off:off + width])
        off += width
    qa, ka, va, qb, kb, vb, cq, ckv, kr, ga, gb, gc = parts

    qa = axial_rope(rms_norm(qa.reshape(b, s, A_HEADS, HEAD_DIM), a_qn_g), row_cs, col_cs)
    ka = axial_rope(rms_norm(ka.reshape(b, s, A_KV_HEADS, HEAD_DIM), a_kn_g), row_cs, col_cs)
    va = va.reshape(b, s, A_KV_HEADS, HEAD_DIM)
    qa = qa.reshape(b, s, A_KV_HEADS, A_HEADS // A_KV_HEADS, HEAD_DIM)
    out_a = dense_block_attention(qa, ka, va, HEAD_DIM ** -0.5).reshape(b, s, A_OUT_W)

    qb = qb.reshape(b, s, B_KV_HEADS, B_HEADS // B_KV_HEADS, HEAD_DIM)
    kb = kb.reshape(b, s, B_KV_HEADS, HEAD_DIM)
    vb = vb.reshape(b, s, B_KV_HEADS, HEAD_DIM)
    sink = b_sink.reshape(B_KV_HEADS, B_HEADS // B_KV_HEADS)
    out_b = windowed_sink_attention(qb, kb, vb, sink, rpb_bias,
                                    HEAD_DIM ** -0.5).reshape(b, s, B_OUT_W)

    q_c = (rms_norm(cq, c_qn_g) @ c_w_uq).reshape(b, s, C_HEADS, C_NOPE_DIM + C_ROPE_DIM)
    q_nope, q_pe = q_c[..., :C_NOPE_DIM], q_c[..., C_NOPE_DIM:]
    q_pe = apply_rope(q_pe, *seq_cs)
    kv_c = (rms_norm(ckv, c_kvn_g) @ c_w_ukv).reshape(b, s, C_HEADS, C_NOPE_DIM + C_V_DIM)
    k_nope, v_c = kv_c[..., :C_NOPE_DIM], kv_c[..., C_NOPE_DIM:]
    k_pe = apply_rope(kr.reshape(b, s, 1, C_ROPE_DIM), *seq_cs)
    k_c = jnp.concatenate([k_nope, jnp.broadcast_to(k_pe, (b, s, C_HEADS, C_ROPE_DIM))], axis=-1)
    q_c = jnp.concatenate([q_nope, q_pe], axis=-1)[:, :, :, None, :]
    out_c = dense_block_attention(q_c, k_c, v_c,
                                  (C_NOPE_DIM + C_ROPE_DIM) ** -0.5).reshape(b, s, C_OUT_W)

    merged = (jax.nn.sigmoid(ga) * (out_a @ w_br_a)
              + jax.nn.sigmoid(gb) * (out_b @ w_br_b)
              + jax.nn.sigmoid(gc) * (out_c @ w_br_c))
    return merged @ w_o


def grouped_moe(h, router_w, router_bias, w_gate, w_up, w_down):
    b, s, d = h.shape
    t = h.reshape(b * s, d)
    scores = jax.nn.sigmoid(jnp.dot(t, router_w, preferred_element_type=jnp.float32))
    biased = scores + router_bias.astype(jnp.float32)
    grouped = biased.reshape(-1, N_GROUPS, EXPERTS_PER_GROUP)
    group_score = jnp.sum(lax.top_k(grouped, TOP_K)[0], axis=-1)
    sel = jnp.argmax(group_score, axis=-1)
    in_group = jnp.take_along_axis(grouped, sel[:, None, None], axis=1)[:, 0]
    _, local = lax.top_k(in_group, TOP_K)
    idx = sel[:, None] * EXPERTS_PER_GROUP + local
    wts = jnp.take_along_axis(scores, idx, axis=1)
    wts = wts / jnp.sum(wts, axis=-1, keepdims=True)
    combine = jnp.sum(jax.nn.one_hot(idx, N_EXPERTS, dtype=jnp.float32) * wts[..., None],
                      axis=1).astype(h.dtype)
    out = jnp.zeros_like(t)
    for e in range(N_EXPERTS):
        hid = jax.nn.silu(t @ w_gate[e]) * (t @ w_up[e])
        out = out + combine[:, e:e + 1] * (hid @ w_down[e])
    return out.reshape(b, s, d)


def setup_inputs(seed: int = 0) -> dict:
    key = jax.random.key(seed)
    ks = jax.random.split(key, 26)
    f32 = jnp.float32

    def nrm(k, shape, scale):
        return jax.random.normal(k, shape, f32) * scale

    L = DEPTH
    return {
        'x': nrm(ks[0], (BATCH, SEQ, D_MODEL), 1.0),
        'ln_in_g': 1.0 + nrm(ks[1], (D_MODEL,), 0.02),
        'ln_in_b': nrm(ks[2], (D_MODEL,), 0.02),
        'w_in': nrm(ks[3], (L, D_MODEL, IN_COLS), D_MODEL ** -0.5),
        'a_q_norm_g': 1.0 + nrm(ks[4], (L, HEAD_DIM), 0.02),
        'a_k_norm_g': 1.0 + nrm(ks[5], (L, HEAD_DIM), 0.02),
        'b_sink': nrm(ks[6], (L, B_HEADS), 0.5),
        'rpb_table': nrm(ks[7], (N_BUCKETS, B_HEADS), 0.5),
        'c_q_norm_g': 1.0 + nrm(ks[8], (L, C_Q_LORA), 0.02),
        'c_kv_norm_g': 1.0 + nrm(ks[9], (L, C_KV_LORA), 0.02),
        'c_w_uq': nrm(ks[10], (L, C_Q_LORA, C_HEADS * (C_NOPE_DIM + C_ROPE_DIM)), C_Q_LORA ** -0.5),
        'c_w_ukv': nrm(ks[11], (L, C_KV_LORA, C_HEADS * (C_NOPE_DIM + C_V_DIM)), C_KV_LORA ** -0.5),
        'w_branch_a': nrm(ks[12], (L, A_OUT_W, D_MODEL), A_OUT_W ** -0.5),
        'w_branch_b': nrm(ks[13], (L, B_OUT_W, D_MODEL), B_OUT_W ** -0.5),
        'w_branch_c': nrm(ks[14], (L, C_OUT_W, D_MODEL), C_OUT_W ** -0.5),
        'w_o': nrm(ks[15], (L, D_MODEL, D_MODEL), D_MODEL ** -0.5 * DEEPNORM_BETA),
        'ln1_g': 1.0 + nrm(ks[16], (L, D_MODEL), 0.02),
        'ln1_b': nrm(ks[17], (L, D_MODEL), 0.02),
        'router_w': nrm(ks[18], (D_MODEL, N_EXPERTS), D_MODEL ** -0.5),
        'router_bias': nrm(ks[19], (N_EXPERTS,), 0.01),
        'w_gate': nrm(ks[20], (L, N_EXPERTS, D_MODEL, D_FF_EXPERT), D_MODEL ** -0.5),
        'w_up': nrm(ks[21], (L, N_EXPERTS, D_MODEL, D_FF_EXPERT), D_MODEL ** -0.5),
        'w_down': nrm(ks[22], (L, N_EXPERTS, D_FF_EXPERT, D_MODEL), D_FF_EXPERT ** -0.5 * DEEPNORM_BETA),
        'ln2_g': 1.0 + nrm(ks[23], (L, D_MODEL), 0.02),
        'ln2_b': nrm(ks[24], (L, D_MODEL), 0.02),
    }


def reference(x, ln_in_g, ln_in_b, w_in, a_q_norm_g, a_k_norm_g, b_sink, rpb_table,
              c_q_norm_g, c_kv_norm_g, c_w_uq, c_w_ukv, w_branch_a, w_branch_b, w_branch_c,
              w_o, ln1_g, ln1_b, router_w, router_bias, w_gate, w_up, w_down, ln2_g, ln2_b):
    b, s, _ = x.shape
    n_rows = s // GRID_W
    rows = jnp.repeat(jnp.arange(n_rows), GRID_W)
    cols = jnp.tile(jnp.arange(GRID_W), n_rows)
    row_cs = rope_cos_sin(rows, HEAD_DIM // 2)
    col_cs = rope_cos_sin(cols, HEAD_DIM // 2)
    seq_cs = rope_cos_sin(jnp.arange(s), C_ROPE_DIM)
    rpb_bias = band_relative_bias(rpb_table)

    x = layer_norm(x, ln_in_g, ln_in_b)
    for l in range(DEPTH):
        mix = token_mixer(x, w_in[l], a_q_norm_g[l], a_k_norm_g[l], b_sink[l], rpb_bias,
                          c_q_norm_g[l], c_kv_norm_g[l], c_w_uq[l], c_w_ukv[l],
                          w_branch_a[l], w_branch_b[l], w_branch_c[l], w_o[l],
                          row_cs, col_cs, seq_cs)
        x = layer_norm(DEEPNORM_ALPHA * x + mix, ln1_g[l], ln1_b[l])
        ffn = grouped_moe(x, router_w, router_bias, w_gate[l], w_up[l], w_down[l])
        x = layer_norm(DEEPNORM_ALPHA * x + ffn, ln2_g[l], ln2_b[l])
    return x
```

```python
import functools
import math

import numpy as np
import jax
import jax.numpy as jnp
from jax import lax
from jax.experimental import pallas as pl
from jax.experimental.pallas import tpu as pltpu

F32 = jnp.float32
BF16 = jnp.bfloat16

D_MODEL = 1024
GRID_W = 64
HEAD_DIM = 64
A_HEADS, A_KV_HEADS = 8, 2
B_HEADS, B_KV_HEADS = 8, 2
WINDOW = 128
C_HEADS = 8
C_NOPE, C_ROPE, C_V = 64, 32, 64
C_Q_LORA, C_KV_LORA = 384, 256
N_BUCKETS, MAX_DISTANCE = 32, 128
N_EXPERTS, N_GROUPS, TOP_K = 16, 4, 2
EXPERTS_PER_GROUP = N_EXPERTS // N_GROUPS
D_FF = 512
ROPE_THETA = 10000.0
NORM_EPS = 1e-6
NEG_INF = -1e30

LANES = 128
VMEM_LIMIT_BYTES = 48 * 1024 * 1024

PROJ_ROWS = 512
ATTN_Q_ROWS = 256
WIN_ROWS = 128
MERGE_ROWS = 256
DISPATCH_ROWS = 1024
EXPERT_ROWS = 256
COMBINE_ROWS = 256

PA_W = A_HEADS * HEAD_DIM + A_KV_HEADS * HEAD_DIM
PA_V = A_KV_HEADS * LANES
PB_W = B_HEADS * HEAD_DIM + 2 * B_KV_HEADS * HEAD_DIM
PC_W = C_Q_LORA + C_KV_LORA + LANES
OFF_A, OFF_AV = 0, PA_W
OFF_B = OFF_AV + PA_V
OFF_C = OFF_B + PB_W
P1_COLS = OFF_C + PC_W


def _cparams(semantics):
    return pltpu.CompilerParams(dimension_semantics=semantics, vmem_limit_bytes=VMEM_LIMIT_BYTES)


def _layer_norm(x, g, b):
    mu = jnp.mean(x, axis=-1, keepdims=True)
    xc = x - mu
    var = jnp.mean(xc * xc, axis=-1, keepdims=True)
    return xc * lax.rsqrt(var + NORM_EPS) * g + b


def _swap16(x):
    lane = lax.broadcasted_iota(jnp.int32, x.shape, x.ndim - 1)
    up = pltpu.roll(x, LANES - 16, axis=x.ndim - 1)
    down = pltpu.roll(x, 16, axis=x.ndim - 1)
    return jnp.where((lane & 16) == 0, up, down)


def _rope_tiles(x, cos, sin):
    tiles = []
    for c in range(x.shape[-1] // LANES):
        xc = x[:, c * LANES:(c + 1) * LANES]
        tiles.append(xc * cos + _swap16(xc) * sin)
    return jnp.concatenate(tiles, axis=-1) if len(tiles) > 1 else tiles[0]


def _proj_kernel(x_ref, lng_ref, lnb_ref, w1_ref, bd_ref, ga_ref, cosa_ref, sina_ref,
                 gcq_ref, gckv_ref, wuq_ref, wukv_ref, cosq_ref, sinq_ref, cosk_ref, sink_ref,
                 vones_a_ref, vones_c_ref,
                 *out_refs, apply_ln):
    if apply_ln:
        xn_ref, qa_ref, ka_ref, va_ref, qb_ref, kb_ref, vb_ref, qc_ref, kc_ref, vc_ref = out_refs
    else:
        qa_ref, ka_ref, va_ref, qb_ref, kb_ref, vb_ref, qc_ref, kc_ref, vc_ref = out_refs
    x = x_ref[...]
    if apply_ln:
        x = _layer_norm(x, lng_ref[...], lnb_ref[...])
        xn_ref[...] = x
    xb = x.astype(BF16)
    p = jnp.dot(xb, w1_ref[...], preferred_element_type=F32)

    qk = p[:, OFF_A:OFF_A + PA_W]
    ms = jnp.dot((qk * qk).astype(BF16), bd_ref[...], preferred_element_type=F32)
    qk = qk * lax.rsqrt(ms + NORM_EPS) * ga_ref[...]
    qk = _rope_tiles(qk, cosa_ref[...], sina_ref[...])
    nq = A_HEADS * HEAD_DIM
    qa_ref[...] = qk[:, :nq].astype(BF16)
    ka_ref[...] = qk[:, nq:].astype(BF16)
    va_ref[...] = (p[:, OFF_AV:OFF_AV + PA_V] + vones_a_ref[...]).astype(BF16)

    nqb = B_HEADS * HEAD_DIM
    nkb = B_KV_HEADS * HEAD_DIM
    qb_ref[...] = p[:, OFF_B:OFF_B + nqb].astype(BF16)
    kb_ref[...] = p[:, OFF_B + nqb:OFF_B + nqb + nkb].astype(BF16)
    vb_ref[...] = p[:, OFF_B + nqb + nkb:OFF_B + PB_W].astype(BF16)

    cq = p[:, OFF_C:OFF_C + C_Q_LORA]
    ckv = p[:, OFF_C + C_Q_LORA:OFF_C + C_Q_LORA + C_KV_LORA]
    kr = p[:, OFF_C + C_Q_LORA + C_KV_LORA:OFF_C + PC_W]
    cqn = cq * lax.rsqrt(jnp.mean(cq * cq, axis=-1, keepdims=True) + NORM_EPS) * gcq_ref[...]
    ckvn = ckv * lax.rsqrt(jnp.mean(ckv * ckv, axis=-1, keepdims=True) + NORM_EPS) * gckv_ref[...]
    qc = jnp.dot(cqn.astype(BF16), wuq_ref[...], preferred_element_type=F32)
    qc_ref[...] = _rope_tiles(qc, cosq_ref[...], sinq_ref[...]).astype(BF16)
    kv = jnp.dot(ckvn.astype(BF16), wukv_ref[...], preferred_element_type=F32)
    kpe = _rope_tiles(kr, cosk_ref[...], sink_ref[...])
    hw = C_HEADS * LANES
    kc_ref[...] = (kv[:, :hw] + jnp.concatenate([kpe] * C_HEADS, axis=-1)).astype(BF16)
    vc_ref[...] = (kv[:, hw:] + vones_c_ref[...]).astype(BF16)


def _proj_call(x, lng, lnb, w1, bd, ga, cosa, sina, gcq, gckv, wuq, wukv, cosq, sinq, cosk, sink,
               vones_a, vones_c, *, seq, apply_ln):
    n = x.shape[0]
    tm = PROJ_ROWS
    nsb = seq // tm
    row = lambda w: pl.BlockSpec((tm, w), lambda i: (i, 0))
    full = lambda a: pl.BlockSpec(a.shape, lambda i: (0,) * a.ndim)
    tab = pl.BlockSpec((tm, LANES), lambda i: (i % nsb, 0))
    out_w = [A_HEADS * HEAD_DIM, A_KV_HEADS * HEAD_DIM, PA_V,
             B_HEADS * HEAD_DIM, B_KV_HEADS * HEAD_DIM, B_KV_HEADS * HEAD_DIM,
             C_HEADS * LANES, C_HEADS * LANES, C_HEADS * LANES]
    out_shape = [jax.ShapeDtypeStruct((n, w), BF16) for w in out_w]
    out_specs = [row(w) for w in out_w]
    if apply_ln:
        out_shape = [jax.ShapeDtypeStruct((n, D_MODEL), F32)] + out_shape
        out_specs = [row(D_MODEL)] + out_specs
    return pl.pallas_call(
        functools.partial(_proj_kernel, apply_ln=apply_ln),
        out_shape=out_shape,
        grid=(n // tm,),
        in_specs=[row(D_MODEL), full(lng), full(lnb), full(w1), full(bd), full(ga), tab, tab,
                  full(gcq), full(gckv), full(wuq), full(wukv), tab, tab, tab, tab,
                  full(vones_a), full(vones_c)],
        out_specs=out_specs,
        compiler_params=_cparams(("parallel",)),
        name="proj_ln" if apply_ln else "proj",
    )(x, lng, lnb, w1, bd, ga, cosa, sina, gcq, gckv, wuq, wukv, cosq, sinq, cosk, sink,
      vones_a, vones_c)


def _dense_attn_kernel(q_ref, k_ref, v_ref, o_ref, *, heads, kv_heads, dqk):
    group = heads // kv_heads
    outs = []
    for h in range(heads):
        kv = h // group
        qh = q_ref[:, h * dqk:(h + 1) * dqk]
        kh = k_ref[:, kv * dqk:(kv + 1) * dqk]
        vh = v_ref[:, kv * LANES:(kv + 1) * LANES]
        s = lax.dot_general(qh, kh, (((1,), (1,)), ((), ())), preferred_element_type=F32)
        m = jnp.max(s, axis=-1, keepdims=True)
        pr = jnp.exp(s - m).astype(BF16)
        oe = jnp.dot(pr, vh, preferred_element_type=F32)
        outs.append(oe[:, :HEAD_DIM] / oe[:, HEAD_DIM:HEAD_DIM + 1])
    o_ref[...] = jnp.concatenate(outs, axis=-1).astype(o_ref.dtype)


def _dense_attn_call(q, k, v, *, batch, seq, heads, kv_heads, dqk, name):
    n = q.shape[0]
    tq = ATTN_Q_ROWS
    nq = seq // tq
    return pl.pallas_call(
        functools.partial(_dense_attn_kernel, heads=heads, kv_heads=kv_heads, dqk=dqk),
        out_shape=jax.ShapeDtypeStruct((n, heads * HEAD_DIM), BF16),
        grid=(batch, nq),
        in_specs=[pl.BlockSpec((tq, heads * dqk), lambda b, i: (b * nq + i, 0)),
                  pl.BlockSpec((seq, kv_heads * dqk), lambda b, i: (b, 0)),
                  pl.BlockSpec((seq, kv_heads * LANES), lambda b, i: (b, 0))],
        out_specs=pl.BlockSpec((tq, heads * HEAD_DIM), lambda b, i: (b * nq + i, 0)),
        compiler_params=_cparams(("parallel", "parallel")),
        name=name,
    )(q, k, v)


def _win_attn_kernel(q_ref, k_ref, v_ref, bias_ref, o_ref, *, heads, kv_heads):
    i = pl.program_id(1)
    nb = pl.num_programs(1)
    group = heads // kv_heads
    tq = WIN_ROWS
    outs = [None] * heads
    for kv in range(kv_heads):
        qs = jnp.concatenate(
            [q_ref[:, (kv * group + g) * HEAD_DIM:(kv * group + g + 1) * HEAD_DIM] for g in range(group)],
            axis=0)
        bias = bias_ref[kv]
        logit_chunks, v_chunks = [], []
        for c in range(3):
            blk = i + (c - 1)
            start = pl.multiple_of(jnp.clip(blk, 0, nb - 1) * tq, tq)
            kc = k_ref[pl.ds(start, tq), kv * HEAD_DIM:(kv + 1) * HEAD_DIM]
            v_chunks.append(v_ref[pl.ds(start, tq), kv * HEAD_DIM:(kv + 1) * HEAD_DIM])
            sc = lax.dot_general(qs, kc, (((1,), (1,)), ((), ())), preferred_element_type=F32)
            sc = sc + bias[:, c * tq:(c + 1) * tq]
            in_range = jnp.logical_and(blk >= 0, blk <= nb - 1)
            logit_chunks.append(jnp.where(in_range, sc, NEG_INF))
        logit_chunks.append(bias[:, 3 * tq:])
        logits = jnp.concatenate(logit_chunks, axis=-1)
        m = jnp.max(logits, axis=-1, keepdims=True)
        pr = jnp.exp(logits - m)
        denom = jnp.sum(pr, axis=-1, keepdims=True)
        vv = jnp.concatenate(v_chunks, axis=0)
        o = jnp.dot(pr[:, :3 * tq].astype(BF16), vv, preferred_element_type=F32) / denom
        for g in range(group):
            outs[kv * group + g] = o[g * tq:(g + 1) * tq]
    o_ref[...] = jnp.concatenate(outs, axis=-1).astype(o_ref.dtype)


def _win_attn_call(q, k, v, bias, *, batch, seq, heads, kv_heads):
    n = q.shape[0]
    tq = WIN_ROWS
    nq = seq // tq
    return pl.pallas_call(
        functools.partial(_win_attn_kernel, heads=heads, kv_heads=kv_heads),
        out_shape=jax.ShapeDtypeStruct((n, heads * HEAD_DIM), BF16),
        grid=(batch, nq),
        in_specs=[pl.BlockSpec((tq, heads * HEAD_DIM), lambda b, i: (b * nq + i, 0)),
                  pl.BlockSpec((seq, kv_heads * HEAD_DIM), lambda b, i: (b, 0)),
                  pl.BlockSpec((seq, kv_heads * HEAD_DIM), lambda b, i: (b, 0)),
                  pl.BlockSpec(bias.shape, lambda b, i: (0, 0, 0))],
        out_specs=pl.BlockSpec((tq, heads * HEAD_DIM), lambda b, i: (b * nq + i, 0)),
        compiler_params=_cparams(("parallel", "parallel")),
        name="win_attn",
    )(q, k, v, bias)


def _pack_rows(x):
    half = x.shape[-1] // 2
    hi = pltpu.bitcast(x[:, :half].astype(BF16).astype(F32), jnp.uint32)
    lo = pltpu.bitcast(x[:, half:].astype(BF16).astype(F32), jnp.uint32)
    return hi | (lo >> 16)


def _unpack_rows(u):
    hi = pltpu.bitcast(u & jnp.uint32(0xFFFF0000), F32)
    lo = pltpu.bitcast(u << 16, F32)
    return jnp.concatenate([hi, lo], axis=-1)


def _merge_kernel(x_ref, oa_ref, ob_ref, oc_ref, wg_ref, wa_ref, wb_ref, wc_ref, wo_ref,
                  g_ref, b_ref, rwt_ref, rb_ref, tri_ref,
                  x1_ref, xp_ref, e_ref, w_ref, r_ref, cnt_ref, carry_ref, *, alpha):
    step = pl.program_id(0)

    @pl.when(step == 0)
    def _():
        carry_ref[...] = jnp.zeros_like(carry_ref)

    x = x_ref[...]
    xb = x.astype(BF16)
    d = D_MODEL
    merged = None
    for idx, (o_ref, wbr_ref) in enumerate(((oa_ref, wa_ref), (ob_ref, wb_ref), (oc_ref, wc_ref))):
        gate = jax.nn.sigmoid(jnp.dot(xb, wg_ref[:, idx * d:(idx + 1) * d], preferred_element_type=F32))
        br = jnp.dot(o_ref[...], wbr_ref[...], preferred_element_type=F32)
        merged = gate * br if merged is None else merged + gate * br
    mix = jnp.dot(merged.astype(BF16), wo_ref[...], preferred_element_type=F32)
    x1 = _layer_norm(alpha * x + mix, g_ref[...], b_ref[...])
    x1_ref[...] = x1
    xp_ref[...] = _pack_rows(x1)

    logits = lax.dot_general(rwt_ref[...], x1, (((1,), (1,)), ((), ())),
                             preferred_element_type=F32, precision=lax.Precision.HIGHEST)
    scores = jax.nn.sigmoid(logits)
    biased = scores + rb_ref[...]
    shape = biased.shape
    rows = lax.broadcasted_iota(jnp.int32, shape, 0)
    sub = lax.broadcasted_iota(jnp.int32, (EXPERTS_PER_GROUP, shape[1]), 0)
    best, sel = None, None
    for g in range(N_GROUPS):
        blk = biased[g * EXPERTS_PER_GROUP:(g + 1) * EXPERTS_PER_GROUP]
        m1 = jnp.max(blk, axis=0, keepdims=True)
        i1 = jnp.min(jnp.where(blk == m1, sub, EXPERTS_PER_GROUP), axis=0, keepdims=True)
        m2 = jnp.max(jnp.where(sub == i1, -jnp.inf, blk), axis=0, keepdims=True)
        gs = m1 + m2
        if best is None:
            best, sel = gs, jnp.zeros_like(i1)
        else:
            better = gs > best
            sel = jnp.where(better, g, sel)
            best = jnp.where(better, gs, best)
    in_group = (rows // EXPERTS_PER_GROUP) == sel
    bm = jnp.where(in_group, biased, -jnp.inf)
    t1 = jnp.max(bm, axis=0, keepdims=True)
    e0 = jnp.min(jnp.where(bm == t1, rows, N_EXPERTS), axis=0, keepdims=True)
    bm2 = jnp.where(rows == e0, -jnp.inf, bm)
    t2 = jnp.max(bm2, axis=0, keepdims=True)
    e1 = jnp.min(jnp.where(bm2 == t2, rows, N_EXPERTS), axis=0, keepdims=True)
    hot0 = rows == e0
    hot1 = rows == e1
    s0 = jnp.sum(jnp.where(hot0, scores, 0.0), axis=0, keepdims=True)
    s1 = jnp.sum(jnp.where(hot1, scores, 0.0), axis=0, keepdims=True)
    tot = s0 + s1
    e_ref[...] = jnp.concatenate([e0, e1], axis=0)
    w_ref[...] = jnp.concatenate([s0 / tot, s1 / tot], axis=0)

    hot = jnp.where(jnp.logical_or(hot0, hot1), 1.0, 0.0)
    prefix = jnp.dot(hot.astype(BF16), tri_ref[...], preferred_element_type=F32) + carry_ref[:, 0:1]
    r0 = jnp.sum(jnp.where(hot0, prefix, 0.0), axis=0, keepdims=True)
    r1 = jnp.sum(jnp.where(hot1, prefix, 0.0), axis=0, keepdims=True)
    r_ref[...] = jnp.concatenate([r0, r1], axis=0).astype(jnp.int32)
    carry_ref[...] = carry_ref[...] + jnp.sum(hot, axis=1, keepdims=True)
    cnt_ref[...] = carry_ref[...]


def _merge_call(x, oa, ob, oc, wg, wa, wb, wc, wo, g, b, rwt, rb, tri, *, alpha):
    n = x.shape[0]
    tm = MERGE_ROWS
    row = lambda w: pl.BlockSpec((tm, w), lambda i: (i, 0))
    full = lambda a: pl.BlockSpec(a.shape, lambda i: (0,) * a.ndim)
    tok = pl.BlockSpec((TOP_K, tm), lambda i: (0, i))
    return pl.pallas_call(
        functools.partial(_merge_kernel, alpha=alpha),
        out_shape=[jax.ShapeDtypeStruct((n, D_MODEL), F32),
                   jax.ShapeDtypeStruct((n, D_MODEL // 2), jnp.uint32),
                   jax.ShapeDtypeStruct((TOP_K, n), jnp.int32),
                   jax.ShapeDtypeStruct((TOP_K, n), F32),
                   jax.ShapeDtypeStruct((TOP_K, n), jnp.int32),
                   jax.ShapeDtypeStruct((N_EXPERTS, LANES), F32)],
        grid=(n // tm,),
        in_specs=[row(D_MODEL), row(oa.shape[1]), row(ob.shape[1]), row(oc.shape[1]),
                  full(wg), full(wa), full(wb), full(wc), full(wo), full(g), full(b),
                  full(rwt), full(rb), full(tri)],
        out_specs=[row(D_MODEL), row(D_MODEL // 2), tok, tok, tok,
                   pl.BlockSpec((N_EXPERTS, LANES), lambda i: (0, 0))],
        scratch_shapes=[pltpu.VMEM((N_EXPERTS, LANES), F32)],
        compiler_params=_cparams(("arbitrary",)),
        name="merge",
    )(x, oa, ob, oc, wg, wa, wb, wc, wo, g, b, rwt, rb, tri)


def _dispatch_kernel(pos_ref, x_hbm, dst_in_hbm, dst_hbm, sem):
    del dst_in_hbm
    base = pl.program_id(0) * DISPATCH_ROWS

    def row_copy(src_row, dst_row):
        return pltpu.make_async_copy(x_hbm.at[pl.ds(src_row, 1)], dst_hbm.at[pl.ds(dst_row, 1)], sem)

    def start(t, carry):
        for j in range(TOP_K):
            row_copy(base + t, pos_ref[0, j, t]).start()
        return carry

    def wait(t, carry):
        for j in range(TOP_K):
            row_copy(base + t, pos_ref[0, j, t]).wait()
        return carry

    lax.fori_loop(0, DISPATCH_ROWS, start, 0)
    lax.fori_loop(0, DISPATCH_ROWS, wait, 0)


def _dispatch_call(pos3, xp, dst0):
    nsteps = pos3.shape[0]
    return pl.pallas_call(
        _dispatch_kernel,
        out_shape=jax.ShapeDtypeStruct(dst0.shape, dst0.dtype),
        grid=(nsteps,),
        in_specs=[pl.BlockSpec((1, TOP_K, DISPATCH_ROWS), lambda i: (i, 0, 0), memory_space=pltpu.SMEM),
                  pl.BlockSpec(memory_space=pl.ANY),
                  pl.BlockSpec(memory_space=pl.ANY)],
        out_specs=pl.BlockSpec(memory_space=pl.ANY),
        scratch_shapes=[pltpu.SemaphoreType.DMA(())],
        input_output_aliases={2: 0},
        compiler_params=pltpu.CompilerParams(dimension_semantics=("arbitrary",), has_side_effects=True),
        name="dispatch",
    )(pos3, xp, dst0)


def _expert_kernel(te_ref, nu_ref, xs_ref, wgu_ref, wd_ref, ys_ref):
    i = pl.program_id(0)

    @pl.when(i < nu_ref[0])
    def _():
        xb = _unpack_rows(xs_ref[...]).astype(BF16)
        gu = jnp.dot(xb, wgu_ref[0], preferred_element_type=F32)
        gate, up = gu[:, :D_FF], gu[:, D_FF:]
        hid = (gate * jax.nn.sigmoid(gate)) * up
        y = jnp.dot(hid.astype(BF16), wd_ref[0], preferred_element_type=F32)
        ys_ref[...] = _pack_rows(y)

    @pl.when(i >= nu_ref[0])
    def _():
        ys_ref[...] = jnp.zeros_like(ys_ref)


def _expert_call(tile_expert, n_used, xs, wgu, wd):
    p = xs.shape[0]
    tm = EXPERT_ROWS
    grid_spec = pltpu.PrefetchScalarGridSpec(
        num_scalar_prefetch=2,
        grid=(p // tm,),
        in_specs=[pl.BlockSpec((tm, D_MODEL // 2), lambda i, te, nu: (i, 0)),
                  pl.BlockSpec((1, D_MODEL, 2 * D_FF), lambda i, te, nu: (te[i], 0, 0)),
                  pl.BlockSpec((1, D_FF, D_MODEL), lambda i, te, nu: (te[i], 0, 0))],
        out_specs=pl.BlockSpec((tm, D_MODEL // 2), lambda i, te, nu: (i, 0)),
    )
    return pl.pallas_call(
        _expert_kernel,
        out_shape=jax.ShapeDtypeStruct((p, D_MODEL // 2), jnp.uint32),
        grid_spec=grid_spec,
        compiler_params=_cparams(("arbitrary",)),
        name="experts",
    )(tile_expert, n_used, xs, wgu, wd)


def _combine_kernel(pos_ref, ys_hbm, x_ref, w_ref, g_ref, b_ref, o_ref, buf_ref, sem, *, alpha):
    tc = COMBINE_ROWS

    def row_copy(t, j):
        return pltpu.make_async_copy(ys_hbm.at[pl.ds(pos_ref[0, j, t], 1)], buf_ref.at[j, pl.ds(t, 1)], sem)

    def start(t, carry):
        for j in range(TOP_K):
            row_copy(t, j).start()
        return carry

    def wait(t, carry):
        for j in range(TOP_K):
            row_copy(t, j).wait()
        return carry

    lax.fori_loop(0, tc, start, 0)
    lax.fori_loop(0, tc, wait, 0)
    w = w_ref[...]
    ffn = w[:, 0:1] * _unpack_rows(buf_ref[0]) + w[:, 1:2] * _unpack_rows(buf_ref[1])
    o_ref[...] = _layer_norm(alpha * x_ref[...] + ffn, g_ref[...], b_ref[...])


def _combine_call(pos3, ys, x1, wts_t, g, b, *, alpha):
    n = x1.shape[0]
    tc = COMBINE_ROWS
    return pl.pallas_call(
        functools.partial(_combine_kernel, alpha=alpha),
        out_shape=jax.ShapeDtypeStruct((n, D_MODEL), F32),
        grid=(n // tc,),
        in_specs=[pl.BlockSpec((1, TOP_K, tc), lambda i: (i, 0, 0), memory_space=pltpu.SMEM),
                  pl.BlockSpec(memory_space=pl.ANY),
                  pl.BlockSpec((tc, D_MODEL), lambda i: (i, 0)),
                  pl.BlockSpec((tc, TOP_K), lambda i: (i, 0)),
                  pl.BlockSpec(g.shape, lambda i: (0, 0)),
                  pl.BlockSpec(b.shape, lambda i: (0, 0))],
        out_specs=pl.BlockSpec((tc, D_MODEL), lambda i: (i, 0)),
        scratch_shapes=[pltpu.VMEM((TOP_K, tc, D_MODEL // 2), jnp.uint32),
                        pltpu.SemaphoreType.DMA(())],
        compiler_params=_cparams(("arbitrary",)),
        name="combine",
    )(pos3, ys, x1, wts_t, g, b)


def _t5_bucket(rel):
    nb = N_BUCKETS // 2
    ret = (rel > 0).astype(np.int32) * nb
    n = np.abs(rel)
    max_exact = nb // 2
    large = max_exact + (np.log(np.maximum(n, 1) / max_exact)
                         / math.log(MAX_DISTANCE / max_exact) * (nb - max_exact)).astype(np.int32)
    large = np.minimum(large, nb - 1)
    return (ret + np.where(n < max_exact, n, large)).astype(np.int32)


def _rope_tables(seq):
    def cs(pos, dim):
        inv = ROPE_THETA ** (-(jnp.arange(0, dim, 2, dtype=F32) / dim))
        ang = pos.astype(F32)[:, None] * inv[None, :]
        return jnp.cos(ang), jnp.sin(ang)

    t = jnp.arange(seq)
    cr, sr = cs(t // GRID_W, HEAD_DIM // 2)
    cc, sc = cs(t % GRID_W, HEAD_DIM // 2)
    cos_head = jnp.concatenate([cr, cr, cc, cc], axis=-1)
    sin_head = jnp.concatenate([-sr, sr, -sc, sc], axis=-1)
    cos_a = jnp.concatenate([cos_head, cos_head], axis=-1)
    sin_a = jnp.concatenate([sin_head, sin_head], axis=-1)
    cs_, ss_ = cs(t, C_ROPE)
    ones = jnp.ones((seq, C_NOPE), F32)
    pad1 = jnp.ones((seq, LANES - C_NOPE - C_ROPE), F32)
    cos_k = jnp.concatenate([ones, cs_, cs_, pad1], axis=-1)
    sin_k = jnp.concatenate([0 * ones, -ss_, ss_, 0 * pad1], axis=-1)
    scale = (C_NOPE + C_ROPE) ** -0.5
    return cos_a, sin_a, cos_k * scale, sin_k * scale, cos_k, sin_k


def _window_bias(rpb_table, b_sink_l):
    span = WIN_ROWS + 2 * WINDOW
    rel = np.arange(span)[None, :] - WINDOW - np.arange(WIN_ROWS)[:, None]
    band = jnp.asarray(np.abs(rel) <= WINDOW)
    bias = rpb_table.astype(F32)[_t5_bucket(rel)]
    bias = jnp.where(band[:, :, None], bias, NEG_INF)
    bias = jnp.transpose(bias, (2, 0, 1))
    sink = jnp.full((B_HEADS, WIN_ROWS, LANES), NEG_INF, F32)
    sink = sink.at[:, :, 0].set(jnp.broadcast_to(b_sink_l.astype(F32)[:, None], (B_HEADS, WIN_ROWS)))
    ext = jnp.concatenate([bias, sink], axis=-1)
    group = B_HEADS // B_KV_HEADS
    return ext.reshape(B_KV_HEADS, group * WIN_ROWS, span + LANES)


def _pad_cols(w, groups, width, total):
    k = w.shape[0]
    w = w.reshape(k, groups, width)
    w = jnp.pad(w, ((0, 0), (0, 0), (0, total - width)))
    return w.reshape(k, groups * total)


def _layer_params(l, w_in, a_q_norm_g, a_k_norm_g, c_q_norm_g, c_kv_norm_g, c_w_uq, c_w_ukv,
                  w_branch_a, w_branch_b, w_branch_c, w_o, w_gate, w_up, w_down):
    w = w_in[l]
    splits = np.cumsum([0, A_HEADS * HEAD_DIM, A_KV_HEADS * HEAD_DIM, A_KV_HEADS * HEAD_DIM,
                        B_HEADS * HEAD_DIM, B_KV_HEADS * HEAD_DIM, B_KV_HEADS * HEAD_DIM,
                        C_Q_LORA, C_KV_LORA, C_ROPE, D_MODEL, D_MODEL, D_MODEL])
    part = [w[:, splits[i]:splits[i + 1]] for i in range(12)]
    qa, ka, va, qb, kb, vb, cq, ckv, kr, ga, gb, gc = part
    va_e = _pad_cols(va, A_KV_HEADS, HEAD_DIM, LANES)
    kr_p = jnp.pad(kr, ((0, 0), (C_NOPE, LANES - C_NOPE - C_ROPE)))
    w1 = jnp.concatenate([qa, ka, va_e, qb * (HEAD_DIM ** -0.5), kb, vb, cq, ckv, kr_p], axis=1).astype(BF16)
    gate_a = jnp.concatenate([jnp.tile(a_q_norm_g[l], A_HEADS) * (HEAD_DIM ** -0.5),
                              jnp.tile(a_k_norm_g[l], A_KV_HEADS)])[None, :].astype(F32)
    wuq = _pad_cols(c_w_uq[l], C_HEADS, C_NOPE + C_ROPE, LANES).astype(BF16)
    ukv = c_w_ukv[l].reshape(C_KV_LORA, C_HEADS, C_NOPE + C_V)
    wuk = jnp.pad(ukv[:, :, :C_NOPE], ((0, 0), (0, 0), (0, LANES - C_NOPE))).reshape(C_KV_LORA, C_HEADS * LANES)
    wuv = jnp.pad(ukv[:, :, C_NOPE:], ((0, 0), (0, 0), (0, LANES - C_V))).reshape(C_KV_LORA, C_HEADS * LANES)
    wukv = jnp.concatenate([wuk, wuv], axis=1).astype(BF16)
    wg = jnp.concatenate([ga, gb, gc], axis=1).astype(BF16)
    wgu = jnp.concatenate([w_gate[l], w_up[l]], axis=-1).astype(BF16)
    return dict(w1=w1, gate_a=gate_a, gcq=c_q_norm_g[l][None, :].astype(F32),
                gckv=c_kv_norm_g[l][None, :].astype(F32), wuq=wuq, wukv=wukv, wg=wg,
                wa=w_branch_a[l].astype(BF16), wb=w_branch_b[l].astype(BF16),
                wc=w_branch_c[l].astype(BF16), wo=w_o[l].astype(BF16),
                wgu=wgu, wd=w_down[l].astype(BF16))


def kernel(x, ln_in_g, ln_in_b, w_in, a_q_norm_g, a_k_norm_g, b_sink, rpb_table, c_q_norm_g, c_kv_norm_g,
           c_w_uq, c_w_ukv, w_branch_a, w_branch_b, w_branch_c, w_o, ln1_g, ln1_b, router_w, router_bias,
           w_gate, w_up, w_down, ln2_g, ln2_b):
    batch, seq, d = x.shape
    depth = w_in.shape[0]
    n = batch * seq
    alpha = (2 * depth) ** 0.25
    assert d == D_MODEL and seq % PROJ_ROWS == 0 and seq % ATTN_Q_ROWS == 0 and seq % WIN_ROWS == 0
    assert n % DISPATCH_ROWS == 0 and n % MERGE_ROWS == 0 and n % COMBINE_ROWS == 0

    cos_a, sin_a, cos_q, sin_q, cos_k, sin_k = _rope_tables(seq)
    seg = np.arange(PA_W) // HEAD_DIM
    bd = jnp.asarray((seg[:, None] == seg[None, :]).astype(np.float32) / HEAD_DIM, BF16)
    ones_pat = np.zeros((1, LANES), np.float32)
    ones_pat[0, HEAD_DIM:] = 1.0
    vones_a = jnp.asarray(np.tile(ones_pat, (1, A_KV_HEADS)))
    vones_c = jnp.asarray(np.tile(ones_pat, (1, C_HEADS)))
    tri = jnp.asarray(np.triu(np.ones((MERGE_ROWS, MERGE_ROWS), np.float32), k=1), BF16)
    rwt = router_w.astype(F32).T
    rb = router_bias.astype(F32)[:, None]
    lng = ln_in_g.astype(F32)[None, :]
    lnb = ln_in_b.astype(F32)[None, :]

    tme = EXPERT_ROWS
    n_tiles = (TOP_K * n) // tme + N_EXPERTS
    p_rows = n_tiles * tme

    h = x.reshape(n, d).astype(F32)
    for l in range(depth):
        prm = _layer_params(l, w_in, a_q_norm_g, a_k_norm_g, c_q_norm_g, c_kv_norm_g, c_w_uq, c_w_ukv,
                            w_branch_a, w_branch_b, w_branch_c, w_o, w_gate, w_up, w_down)
        outs = _proj_call(h, lng, lnb, prm["w1"], bd, prm["gate_a"], cos_a, sin_a, prm["gcq"], prm["gckv"],
                          prm["wuq"], prm["wukv"], cos_q, sin_q, cos_k, sin_k, vones_a, vones_c,
                          seq=seq, apply_ln=(l == 0))
        if l == 0:
            h, *outs = outs
        qa, ka, va, qb, kb, vb, qc, kc, vc = outs
        oa = _dense_attn_call(qa, ka, va, batch=batch, seq=seq, heads=A_HEADS, kv_heads=A_KV_HEADS,
                              dqk=HEAD_DIM, name="attn_a")
        bias = _window_bias(rpb_table, b_sink[l])
        ob = _win_attn_call(qb, kb, vb, bias, batch=batch, seq=seq, heads=B_HEADS, kv_heads=B_KV_HEADS)
        oc = _dense_attn_call(qc, kc, vc, batch=batch, seq=seq, heads=C_HEADS, kv_heads=C_HEADS,
                              dqk=LANES, name="attn_c")
        x1, xp, e_idx, wts, rank, cnt = _merge_call(
            h, oa, ob, oc, prm["wg"], prm["wa"], prm["wb"], prm["wc"], prm["wo"],
            ln1_g[l].astype(F32)[None, :], ln1_b[l].astype(F32)[None, :], rwt, rb, tri, alpha=alpha)

        counts = cnt[:, 0].astype(jnp.int32)
        padded = ((counts + tme - 1) // tme) * tme
        ends = jnp.cumsum(padded)
        offs = ends - padded
        pos = jnp.take(offs, e_idx, axis=0) + rank
        tile_start = jnp.arange(n_tiles, dtype=jnp.int32) * tme
        tile_expert = jnp.minimum(jnp.sum(tile_start[:, None] >= ends[None, :], axis=1), N_EXPERTS - 1)
        tile_expert = tile_expert.astype(jnp.int32)
        n_used = (ends[-1:] // tme).astype(jnp.int32)

        pos_d = pos.reshape(TOP_K, n // DISPATCH_ROWS, DISPATCH_ROWS).transpose(1, 0, 2)
        xs = _dispatch_call(pos_d, xp, jnp.zeros((p_rows, D_MODEL // 2), jnp.uint32))
        ys = _expert_call(tile_expert, n_used, xs, prm["wgu"], prm["wd"])
        pos_c = pos.reshape(TOP_K, n // COMBINE_ROWS, COMBINE_ROWS).transpose(1, 0, 2)
        h = _combine_call(pos_c, ys, x1, wts.T, ln2_g[l].astype(F32)[None, :], ln2_b[l].astype(F32)[None, :],
                          alpha=alpha)
    return h.reshape(batch, seq, d).astype(x.dtype)
```

```python
import functools
import math

import numpy as np
import jax
import jax.numpy as jnp
from jax import lax
from jax.experimental import pallas as pl
from jax.experimental.pallas import tpu as pltpu

F32 = jnp.float32
BF16 = jnp.bfloat16

D_MODEL = 1024
GRID_W = 64
HEAD_DIM = 64
A_HEADS, A_KV_HEADS = 8, 2
B_HEADS, B_KV_HEADS = 8, 2
WINDOW = 128
C_HEADS = 8
C_NOPE, C_ROPE, C_V = 64, 32, 64
C_Q_LORA, C_KV_LORA = 384, 256
N_BUCKETS, MAX_DISTANCE = 32, 128
N_EXPERTS, N_GROUPS, TOP_K = 16, 4, 2
EXPERTS_PER_GROUP = N_EXPERTS // N_GROUPS
D_FF = 512
ROPE_THETA = 10000.0
NORM_EPS = 1e-6
NEG_INF = -1e30

LANES = 128
ROW_TILES = D_MODEL // 2 // LANES
BF16_SUBLANES = 16
VT_ROWS = HEAD_DIM + BF16_SUBLANES
LOG2E = math.log2(math.e)
VMEM_LIMIT_BYTES = 48 * 1024 * 1024

PROJ_ROWS = 512
ATTN_Q_ROWS = 512
WIN_ROWS = 128
MERGE_ROWS = 512
DISPATCH_ROWS = 1024
EXPERT_ROWS = 256
COMBINE_ROWS = 256
DMA_UNROLL = 8

PA_W = A_HEADS * HEAD_DIM + A_KV_HEADS * HEAD_DIM
PA_V = A_KV_HEADS * LANES
PB_W = B_HEADS * HEAD_DIM + 2 * B_KV_HEADS * HEAD_DIM
PC_W = C_Q_LORA + C_KV_LORA + LANES
OFF_A, OFF_AV = 0, PA_W
OFF_B = OFF_AV + PA_V
OFF_C = OFF_B + PB_W
P1_COLS = OFF_C + PC_W


def _cparams(semantics):
    return pltpu.CompilerParams(dimension_semantics=semantics, vmem_limit_bytes=VMEM_LIMIT_BYTES)


def _layer_norm(x, g, b):
    mu = jnp.mean(x, axis=-1, keepdims=True)
    xc = x - mu
    var = jnp.mean(xc * xc, axis=-1, keepdims=True)
    return xc * lax.rsqrt(var + NORM_EPS) * g + b


def _swap16(x):
    lane = lax.broadcasted_iota(jnp.int32, x.shape, x.ndim - 1)
    up = pltpu.roll(x, LANES - 16, axis=x.ndim - 1)
    down = pltpu.roll(x, 16, axis=x.ndim - 1)
    return jnp.where((lane & 16) == 0, up, down)


def _rope_tiles(x, cos, sin):
    tiles = []
    for c in range(x.shape[-1] // LANES):
        xc = x[:, c * LANES:(c + 1) * LANES]
        tiles.append(xc * cos + _swap16(xc) * sin)
    return jnp.concatenate(tiles, axis=-1) if len(tiles) > 1 else tiles[0]


def _store_vt(vt_ref, v_ext, kv_heads):
    vt = v_ext.T
    for kv in range(kv_heads):
        vt_ref[kv * VT_ROWS:(kv + 1) * VT_ROWS, :] = vt[kv * LANES:kv * LANES + VT_ROWS].astype(BF16)


def _proj_kernel(x_ref, lng_ref, lnb_ref, w1_ref, bd_ref, ga_ref, cosa_ref, sina_ref,
                 gcq_ref, gckv_ref, wuq_ref, wukv_ref, cosq_ref, sinq_ref, cosk_ref, sink_ref,
                 vones_a_ref, vones_c_ref,
                 *out_refs, apply_ln):
    if apply_ln:
        xn_ref, qa_ref, ka_ref, va_ref, qb_ref, kb_ref, vb_ref, qc_ref, kc_ref, vc_ref = out_refs
    else:
        qa_ref, ka_ref, va_ref, qb_ref, kb_ref, vb_ref, qc_ref, kc_ref, vc_ref = out_refs
    x = x_ref[...]
    if apply_ln:
        x = _layer_norm(x, lng_ref[...], lnb_ref[...])
        xn_ref[...] = x
    xb = x.astype(BF16)
    p = jnp.dot(xb, w1_ref[...], preferred_element_type=F32)

    qk = p[:, OFF_A:OFF_A + PA_W]
    ms = jnp.dot((qk * qk).astype(BF16), bd_ref[...], preferred_element_type=F32)
    qk = qk * lax.rsqrt(ms + NORM_EPS) * ga_ref[...]
    qk = _rope_tiles(qk, cosa_ref[...], sina_ref[...])
    nq = A_HEADS * HEAD_DIM
    qa_ref[...] = qk[:, :nq].astype(BF16)
    kk = qk[:, nq:]
    low = lax.broadcasted_iota(jnp.int32, kk.shape, 1) < HEAD_DIM
    ka_ref[...] = jnp.concatenate([jnp.where(low, kk, 0.0),
                                   jnp.where(low, pltpu.roll(kk, HEAD_DIM, axis=1), 0.0)],
                                  axis=-1).astype(BF16)
    _store_vt(va_ref, p[:, OFF_AV:OFF_AV + PA_V] + vones_a_ref[...], A_KV_HEADS)

    nqb = B_HEADS * HEAD_DIM
    nkb = B_KV_HEADS * HEAD_DIM
    qb_ref[...] = p[:, OFF_B:OFF_B + nqb].astype(BF16)
    kb_ref[...] = p[:, OFF_B + nqb:OFF_B + nqb + nkb].astype(BF16)
    vb_ref[...] = p[:, OFF_B + nqb + nkb:OFF_B + PB_W].astype(BF16)

    cq = p[:, OFF_C:OFF_C + C_Q_LORA]
    ckv = p[:, OFF_C + C_Q_LORA:OFF_C + C_Q_LORA + C_KV_LORA]
    kr = p[:, OFF_C + C_Q_LORA + C_KV_LORA:OFF_C + PC_W]
    cqn = cq * lax.rsqrt(jnp.mean(cq * cq, axis=-1, keepdims=True) + NORM_EPS) * gcq_ref[...]
    ckvn = ckv * lax.rsqrt(jnp.mean(ckv * ckv, axis=-1, keepdims=True) + NORM_EPS) * gckv_ref[...]
    qc = jnp.dot(cqn.astype(BF16), wuq_ref[...], preferred_element_type=F32)
    qc_ref[...] = _rope_tiles(qc, cosq_ref[...], sinq_ref[...]).astype(BF16)
    kv = jnp.dot(ckvn.astype(BF16), wukv_ref[...], preferred_element_type=F32)
    kpe = _rope_tiles(kr, cosk_ref[...], sink_ref[...])
    hw = C_HEADS * LANES
    kc_ref[...] = (kv[:, :hw] + jnp.concatenate([kpe] * C_HEADS, axis=-1)).astype(BF16)
    _store_vt(vc_ref, kv[:, hw:] + vones_c_ref[...], C_HEADS)


def _proj_call(x, lng, lnb, w1, bd, ga, cosa, sina, gcq, gckv, wuq, wukv, cosq, sinq, cosk, sink,
               vones_a, vones_c, *, seq, apply_ln):
    n = x.shape[0]
    tm = PROJ_ROWS
    nsb = seq // tm
    row = lambda w: pl.BlockSpec((tm, w), lambda i: (i, 0))
    full = lambda a: pl.BlockSpec(a.shape, lambda i: (0,) * a.ndim)
    tab = pl.BlockSpec((tm, LANES), lambda i: (i % nsb, 0))
    out_w = [A_HEADS * HEAD_DIM, A_KV_HEADS * LANES, -A_KV_HEADS * VT_ROWS,
             B_HEADS * HEAD_DIM, B_KV_HEADS * HEAD_DIM, B_KV_HEADS * HEAD_DIM,
             C_HEADS * LANES, C_HEADS * LANES, -C_HEADS * VT_ROWS]
    out_shape = [jax.ShapeDtypeStruct((n, w) if w > 0 else (-w, n), BF16) for w in out_w]
    out_specs = [row(w) if w > 0 else pl.BlockSpec((-w, tm), lambda i: (0, i)) for w in out_w]
    if apply_ln:
        out_shape = [jax.ShapeDtypeStruct((n, D_MODEL), F32)] + out_shape
        out_specs = [row(D_MODEL)] + out_specs
    return pl.pallas_call(
        functools.partial(_proj_kernel, apply_ln=apply_ln),
        out_shape=out_shape,
        grid=(n // tm,),
        in_specs=[row(D_MODEL), full(lng), full(lnb), full(w1), full(bd), full(ga), tab, tab,
                  full(gcq), full(gckv), full(wuq), full(wukv), tab, tab, tab, tab,
                  full(vones_a), full(vones_c)],
        out_specs=out_specs,
        compiler_params=_cparams(("parallel",)),
        name="proj_ln" if apply_ln else "proj",
    )(x, lng, lnb, w1, bd, ga, cosa, sina, gcq, gckv, wuq, wukv, cosq, sinq, cosk, sink,
      vones_a, vones_c)


def _dense_attn_kernel(q_ref, k_ref, vt_ref, o_ref, *, heads, kv_heads, dq):
    group = heads // kv_heads

    def scores_t(h):
        kv = h // group
        qh = q_ref[:, h * dq:(h + 1) * dq]
        kh = k_ref[:, kv * LANES:(kv + 1) * LANES]
        if dq < LANES:
            kh = kh[:, :dq]
        return lax.dot_general(kh, qh, (((1,), (1,)), ((), ())), preferred_element_type=F32)

    outs = []
    st = {h: scores_t(h) for h in range(min(2, heads))}
    for h in range(heads):
        kv = h // group
        s = st.pop(h)
        m = jnp.max(s, axis=0, keepdims=True)
        pt = jnp.exp2(s - m).astype(BF16)
        if h + 2 < heads:
            st[h + 2] = scores_t(h + 2)
        vth = vt_ref[kv * VT_ROWS:(kv + 1) * VT_ROWS, :]
        ot = jnp.dot(vth, pt, preferred_element_type=F32)
        outs.append(ot[:HEAD_DIM] / ot[HEAD_DIM:HEAD_DIM + 1])
    o_ref[...] = jnp.concatenate(outs, axis=0).T.astype(o_ref.dtype)


def _dense_attn_call(q, k, vt, *, batch, seq, heads, kv_heads, dq, name):
    n = q.shape[0]
    tq = ATTN_Q_ROWS
    nq = seq // tq
    return pl.pallas_call(
        functools.partial(_dense_attn_kernel, heads=heads, kv_heads=kv_heads, dq=dq),
        out_shape=jax.ShapeDtypeStruct((n, heads * HEAD_DIM), BF16),
        grid=(batch, nq),
        in_specs=[pl.BlockSpec((tq, heads * dq), lambda b, i: (b * nq + i, 0)),
                  pl.BlockSpec((seq, kv_heads * LANES), lambda b, i: (b, 0)),
                  pl.BlockSpec((kv_heads * VT_ROWS, seq), lambda b, i: (0, b))],
        out_specs=pl.BlockSpec((tq, heads * HEAD_DIM), lambda b, i: (b * nq + i, 0)),
        compiler_params=_cparams(("parallel", "parallel")),
        name=name,
    )(q, k, vt)


def _win_attn_kernel(q_ref, k_ref, v_ref, bias_ref, o_ref, *, heads, kv_heads):
    i = pl.program_id(1)
    nb = pl.num_programs(1)
    group = heads // kv_heads
    tq = WIN_ROWS
    outs = [None] * heads
    for kv in range(kv_heads):
        qs = jnp.concatenate(
            [q_ref[:, (kv * group + g) * HEAD_DIM:(kv * group + g + 1) * HEAD_DIM] for g in range(group)],
            axis=0)
        bias = bias_ref[kv]
        logit_chunks, v_chunks = [], []
        for c in range(3):
            blk = i + (c - 1)
            start = pl.multiple_of(jnp.clip(blk, 0, nb - 1) * tq, tq)
            kc = k_ref[pl.ds(start, tq), kv * HEAD_DIM:(kv + 1) * HEAD_DIM]
            v_chunks.append(v_ref[pl.ds(start, tq), kv * HEAD_DIM:(kv + 1) * HEAD_DIM])
            sc = lax.dot_general(qs, kc, (((1,), (1,)), ((), ())), preferred_element_type=F32)
            sc = sc + bias[:, c * tq:(c + 1) * tq]
            in_range = jnp.logical_and(blk >= 0, blk <= nb - 1)
            logit_chunks.append(jnp.where(in_range, sc, NEG_INF))
        logit_chunks.append(bias[:, 3 * tq:])
        logits = jnp.concatenate(logit_chunks, axis=-1)
        m = jnp.max(logits, axis=-1, keepdims=True)
        pr = jnp.exp(logits - m)
        denom = jnp.sum(pr, axis=-1, keepdims=True)
        vv = jnp.concatenate(v_chunks, axis=0)
        o = jnp.dot(pr[:, :3 * tq].astype(BF16), vv, preferred_element_type=F32) / denom
        for g in range(group):
            outs[kv * group + g] = o[g * tq:(g + 1) * tq]
    o_ref[...] = jnp.concatenate(outs, axis=-1).astype(o_ref.dtype)


def _win_attn_call(q, k, v, bias, *, batch, seq, heads, kv_heads):
    n = q.shape[0]
    tq = WIN_ROWS
    nq = seq // tq
    return pl.pallas_call(
        functools.partial(_win_attn_kernel, heads=heads, kv_heads=kv_heads),
        out_shape=jax.ShapeDtypeStruct((n, heads * HEAD_DIM), BF16),
        grid=(batch, nq),
        in_specs=[pl.BlockSpec((tq, heads * HEAD_DIM), lambda b, i: (b * nq + i, 0)),
                  pl.BlockSpec((seq, kv_heads * HEAD_DIM), lambda b, i: (b, 0)),
                  pl.BlockSpec((seq, kv_heads * HEAD_DIM), lambda b, i: (b, 0)),
                  pl.BlockSpec(bias.shape, lambda b, i: (0, 0, 0))],
        out_specs=pl.BlockSpec((tq, heads * HEAD_DIM), lambda b, i: (b * nq + i, 0)),
        compiler_params=_cparams(("parallel", "parallel")),
        name="win_attn",
    )(q, k, v, bias)


def _pack_rows(x):
    half = x.shape[-1] // 2
    hi = pltpu.bitcast(x[:, :half].astype(BF16).astype(F32), jnp.uint32)
    lo = pltpu.bitcast(x[:, half:].astype(BF16).astype(F32), jnp.uint32)
    return hi | (lo >> 16)


def _unpack_rows(u):
    hi = pltpu.bitcast(u & jnp.uint32(0xFFFF0000), F32)
    lo = pltpu.bitcast(u << 16, F32)
    return jnp.concatenate([hi, lo], axis=-1)


def _store_row_tiles(ref, u):
    for c in range(ROW_TILES):
        ref[:, c, :] = u[:, c * LANES:(c + 1) * LANES]


def _load_row_tiles(ref):
    return jnp.concatenate([ref[:, c, :] for c in range(ROW_TILES)], axis=-1)


def _merge_kernel(x_ref, oa_ref, ob_ref, oc_ref, wg_ref, wa_ref, wb_ref, wc_ref, wo_ref,
                  g_ref, b_ref, rwt_ref, rb_ref, tri_ref,
                  x1_ref, xp_ref, e_ref, w_ref, r_ref, cnt_ref, carry_ref, *, alpha):
    step = pl.program_id(0)

    @pl.when(step == 0)
    def _():
        carry_ref[...] = jnp.zeros_like(carry_ref)

    x = x_ref[...]
    xb = x.astype(BF16)
    d = D_MODEL
    merged = None
    for idx, (o_ref, wbr_ref) in enumerate(((oa_ref, wa_ref), (ob_ref, wb_ref), (oc_ref, wc_ref))):
        gate = jax.nn.sigmoid(jnp.dot(xb, wg_ref[:, idx * d:(idx + 1) * d], preferred_element_type=F32))
        br = jnp.dot(o_ref[...], wbr_ref[...], preferred_element_type=F32)
        merged = gate * br if merged is None else merged + gate * br
    mix = jnp.dot(merged.astype(BF16), wo_ref[...], preferred_element_type=F32)
    x1 = _layer_norm(alpha * x + mix, g_ref[...], b_ref[...])
    x1_ref[...] = x1
    _store_row_tiles(xp_ref, _pack_rows(x1))

    logits = lax.dot_general(rwt_ref[...], x1, (((1,), (1,)), ((), ())),
                             preferred_element_type=F32, precision=lax.Precision.HIGHEST)
    scores = jax.nn.sigmoid(logits)
    biased = scores + rb_ref[...]
    shape = biased.shape
    rows = lax.broadcasted_iota(jnp.int32, shape, 0)
    sub = lax.broadcasted_iota(jnp.int32, (EXPERTS_PER_GROUP, shape[1]), 0)
    best, sel = None, None
    for g in range(N_GROUPS):
        blk = biased[g * EXPERTS_PER_GROUP:(g + 1) * EXPERTS_PER_GROUP]
        m1 = jnp.max(blk, axis=0, keepdims=True)
        i1 = jnp.min(jnp.where(blk == m1, sub, EXPERTS_PER_GROUP), axis=0, keepdims=True)
        m2 = jnp.max(jnp.where(sub == i1, -jnp.inf, blk), axis=0, keepdims=True)
        gs = m1 + m2
        if best is None:
            best, sel = gs, jnp.zeros_like(i1)
        else:
            better = gs > best
            sel = jnp.where(better, g, sel)
            best = jnp.where(better, gs, best)
    in_group = (rows >> 2) == sel
    bm = jnp.where(in_group, biased, -jnp.inf)
    t1 = jnp.max(bm, axis=0, keepdims=True)
    e0 = jnp.min(jnp.where(bm == t1, rows, N_EXPERTS), axis=0, keepdims=True)
    bm2 = jnp.where(rows == e0, -jnp.inf, bm)
    t2 = jnp.max(bm2, axis=0, keepdims=True)
    e1 = jnp.min(jnp.where(bm2 == t2, rows, N_EXPERTS), axis=0, keepdims=True)
    hot0 = rows == e0
    hot1 = rows == e1
    s0 = jnp.sum(jnp.where(hot0, scores, 0.0), axis=0, keepdims=True)
    s1 = jnp.sum(jnp.where(hot1, scores, 0.0), axis=0, keepdims=True)
    tot = s0 + s1
    e_ref[...] = jnp.concatenate([e0, e1], axis=0)
    w_ref[...] = jnp.concatenate([s0 / tot, s1 / tot], axis=0)

    hot = jnp.where(jnp.logical_or(hot0, hot1), 1.0, 0.0)
    prefix = jnp.dot(hot.astype(BF16), tri_ref[...], preferred_element_type=F32) + carry_ref[:, 0:1]
    r0 = jnp.sum(jnp.where(hot0, prefix, 0.0), axis=0, keepdims=True)
    r1 = jnp.sum(jnp.where(hot1, prefix, 0.0), axis=0, keepdims=True)
    r_ref[...] = jnp.concatenate([r0, r1], axis=0).astype(jnp.int32)
    carry_ref[...] = carry_ref[...] + jnp.sum(hot, axis=1, keepdims=True)
    cnt_ref[...] = carry_ref[...]


def _merge_call(x, oa, ob, oc, wg, wa, wb, wc, wo, g, b, rwt, rb, tri, *, alpha):
    n = x.shape[0]
    tm = MERGE_ROWS
    row = lambda w: pl.BlockSpec((tm, w), lambda i: (i, 0))
    full = lambda a: pl.BlockSpec(a.shape, lambda i: (0,) * a.ndim)
    tok = pl.BlockSpec((TOP_K, tm), lambda i: (0, i))
    packed = pl.BlockSpec((tm, ROW_TILES, LANES), lambda i: (i, 0, 0))
    return pl.pallas_call(
        functools.partial(_merge_kernel, alpha=alpha),
        out_shape=[jax.ShapeDtypeStruct((n, D_MODEL), F32),
                   jax.ShapeDtypeStruct((n, ROW_TILES, LANES), jnp.uint32),
                   jax.ShapeDtypeStruct((TOP_K, n), jnp.int32),
                   jax.ShapeDtypeStruct((TOP_K, n), F32),
                   jax.ShapeDtypeStruct((TOP_K, n), jnp.int32),
                   jax.ShapeDtypeStruct((N_EXPERTS, LANES), F32)],
        grid=(n // tm,),
        in_specs=[row(D_MODEL), row(oa.shape[1]), row(ob.shape[1]), row(oc.shape[1]),
                  full(wg), full(wa), full(wb), full(wc), full(wo), full(g), full(b),
                  full(rwt), full(rb), full(tri)],
        out_specs=[row(D_MODEL), packed, tok, tok, tok,
                   pl.BlockSpec((N_EXPERTS, LANES), lambda i: (0, 0))],
        scratch_shapes=[pltpu.VMEM((N_EXPERTS, LANES), F32)],
        compiler_params=_cparams(("arbitrary",)),
        name="merge",
    )(x, oa, ob, oc, wg, wa, wb, wc, wo, g, b, rwt, rb, tri)


def _dispatch_kernel(pos_ref, x_ref, dst_in_hbm, dst_hbm, sem):
    del dst_in_hbm
    tb = DISPATCH_ROWS

    def start(g, carry):
        for u in range(DMA_UNROLL):
            t = g * DMA_UNROLL + u
            for j in range(TOP_K):
                pltpu.make_async_copy(x_ref.at[t], dst_hbm.at[pos_ref[0, 0, j * tb + t]], sem).start()
        return carry

    lax.fori_loop(0, tb // DMA_UNROLL, start, 0)
    for j in range(TOP_K):
        pltpu.make_async_copy(x_ref, dst_hbm.at[pl.ds(0, tb)], sem).wait()


def _dispatch_call(pos_flat, xp, dst0):
    nsteps = pos_flat.shape[0]
    tb = DISPATCH_ROWS
    return pl.pallas_call(
        _dispatch_kernel,
        out_shape=jax.ShapeDtypeStruct(dst0.shape, dst0.dtype),
        grid=(nsteps,),
        in_specs=[pl.BlockSpec((1, 1, TOP_K * tb), lambda i: (i, 0, 0), memory_space=pltpu.SMEM),
                  pl.BlockSpec((tb, ROW_TILES, LANES), lambda i: (i, 0, 0)),
                  pl.BlockSpec(memory_space=pl.ANY)],
        out_specs=pl.BlockSpec(memory_space=pl.ANY),
        scratch_shapes=[pltpu.SemaphoreType.DMA(())],
        input_output_aliases={2: 0},
        compiler_params=pltpu.CompilerParams(dimension_semantics=("arbitrary",), has_side_effects=True,
                                             vmem_limit_bytes=VMEM_LIMIT_BYTES),
        name="dispatch",
    )(pos_flat, xp, dst0)


def _expert_kernel(te_ref, nu_ref, xs_ref, wgu_ref, wd_ref, ys_ref):
    i = pl.program_id(0)

    @pl.when(i < nu_ref[0])
    def _():
        xb = _unpack_rows(_load_row_tiles(xs_ref)).astype(BF16)
        gu = jnp.dot(xb, wgu_ref[0], preferred_element_type=F32)
        gate, up = gu[:, :D_FF], gu[:, D_FF:]
        hid = (gate * jax.nn.sigmoid(gate)) * up
        y = jnp.dot(hid.astype(BF16), wd_ref[0], preferred_element_type=F32)
        _store_row_tiles(ys_ref, _pack_rows(y))

    @pl.when(i >= nu_ref[0])
    def _():
        ys_ref[...] = jnp.zeros_like(ys_ref)


def _expert_call(tile_expert, n_used, xs, wgu, wd):
    p = xs.shape[0]
    tm = EXPERT_ROWS
    grid_spec = pltpu.PrefetchScalarGridSpec(
        num_scalar_prefetch=2,
        grid=(p // tm,),
        in_specs=[pl.BlockSpec((tm, ROW_TILES, LANES), lambda i, te, nu: (i, 0, 0)),
                  pl.BlockSpec((1, D_MODEL, 2 * D_FF), lambda i, te, nu: (te[i], 0, 0)),
                  pl.BlockSpec((1, D_FF, D_MODEL), lambda i, te, nu: (te[i], 0, 0))],
        out_specs=pl.BlockSpec((tm, ROW_TILES, LANES), lambda i, te, nu: (i, 0, 0)),
    )
    return pl.pallas_call(
        _expert_kernel,
        out_shape=jax.ShapeDtypeStruct((p, ROW_TILES, LANES), jnp.uint32),
        grid_spec=grid_spec,
        compiler_params=_cparams(("arbitrary",)),
        name="experts",
    )(tile_expert, n_used, xs, wgu, wd)


def _combine_kernel(pos_ref, nxt_ref, ys_hbm, x_ref, w_ref, g_ref, b_ref, o_ref, buf_ref, sem, *, alpha):
    tc = COMBINE_ROWS
    i = pl.program_id(0)
    slot = i % 2

    def gather(idx_ref, to_slot):
        def start(g, carry):
            for u in range(DMA_UNROLL):
                t = g * DMA_UNROLL + u
                for j in range(TOP_K):
                    pltpu.make_async_copy(ys_hbm.at[idx_ref[0, 0, j * tc + t]], buf_ref.at[to_slot, j, t],
                                          sem.at[to_slot]).start()
            return carry
        lax.fori_loop(0, tc // DMA_UNROLL, start, 0)

    @pl.when(i == 0)
    def _():
        gather(pos_ref, 0)

    @pl.when(i + 1 < pl.num_programs(0))
    def _():
        gather(nxt_ref, 1 - slot)

    for j in range(TOP_K):
        pltpu.make_async_copy(ys_hbm.at[pl.ds(0, tc)], buf_ref.at[slot, j], sem.at[slot]).wait()
    w = w_ref[...]
    ffn = (w[:, 0:1] * _unpack_rows(_load_row_tiles(buf_ref.at[slot, 0]))
           + w[:, 1:2] * _unpack_rows(_load_row_tiles(buf_ref.at[slot, 1])))
    o_ref[...] = _layer_norm(alpha * x_ref[...] + ffn, g_ref[...], b_ref[...])


def _combine_call(pos_flat, ys, x1, wts_t, g, b, *, alpha):
    n = x1.shape[0]
    tc = COMBINE_ROWS
    nsteps = n // tc
    idx = lambda shift: pl.BlockSpec((1, 1, TOP_K * tc), lambda i: (jnp.minimum(i + shift, nsteps - 1), 0, 0),
                                     memory_space=pltpu.SMEM)
    return pl.pallas_call(
        functools.partial(_combine_kernel, alpha=alpha),
        out_shape=jax.ShapeDtypeStruct((n, D_MODEL), F32),
        grid=(nsteps,),
        in_specs=[idx(0), idx(1),
                  pl.BlockSpec(memory_space=pl.ANY),
                  pl.BlockSpec((tc, D_MODEL), lambda i: (i, 0)),
                  pl.BlockSpec((tc, TOP_K), lambda i: (i, 0)),
                  pl.BlockSpec(g.shape, lambda i: (0, 0)),
                  pl.BlockSpec(b.shape, lambda i: (0, 0))],
        out_specs=pl.BlockSpec((tc, D_MODEL), lambda i: (i, 0)),
        scratch_shapes=[pltpu.VMEM((2, TOP_K, tc, ROW_TILES, LANES), jnp.uint32),
                        pltpu.SemaphoreType.DMA((2,))],
        compiler_params=_cparams(("arbitrary",)),
        name="combine",
    )(pos_flat, pos_flat, ys, x1, wts_t, g, b)


def _t5_bucket(rel):
    nb = N_BUCKETS // 2
    ret = (rel > 0).astype(np.int32) * nb
    n = np.abs(rel)
    max_exact = nb // 2
    large = max_exact + (np.log(np.maximum(n, 1) / max_exact)
                         / math.log(MAX_DISTANCE / max_exact) * (nb - max_exact)).astype(np.int32)
    large = np.minimum(large, nb - 1)
    return (ret + np.where(n < max_exact, n, large)).astype(np.int32)


def _rope_tables(seq):
    def cs(pos, dim):
        inv = ROPE_THETA ** (-(jnp.arange(0, dim, 2, dtype=F32) / dim))
        ang = pos.astype(F32)[:, None] * inv[None, :]
        return jnp.cos(ang), jnp.sin(ang)

    t = jnp.arange(seq)
    cr, sr = cs(t // GRID_W, HEAD_DIM // 2)
    cc, sc = cs(t % GRID_W, HEAD_DIM // 2)
    cos_head = jnp.concatenate([cr, cr, cc, cc], axis=-1)
    sin_head = jnp.concatenate([-sr, sr, -sc, sc], axis=-1)
    cos_a = jnp.concatenate([cos_head, cos_head], axis=-1)
    sin_a = jnp.concatenate([sin_head, sin_head], axis=-1)
    cs_, ss_ = cs(t, C_ROPE)
    ones = jnp.ones((seq, C_NOPE), F32)
    pad1 = jnp.ones((seq, LANES - C_NOPE - C_ROPE), F32)
    cos_k = jnp.concatenate([ones, cs_, cs_, pad1], axis=-1)
    sin_k = jnp.concatenate([0 * ones, -ss_, ss_, 0 * pad1], axis=-1)
    scale = (C_NOPE + C_ROPE) ** -0.5 * LOG2E
    return cos_a, sin_a, cos_k * scale, sin_k * scale, cos_k, sin_k


def _window_bias(rpb_table, b_sink_l):
    span = WIN_ROWS + 2 * WINDOW
    rel = np.arange(span)[None, :] - WINDOW - np.arange(WIN_ROWS)[:, None]
    band = jnp.asarray(np.abs(rel) <= WINDOW)
    onehot = jnp.asarray(_t5_bucket(rel)[:, :, None] == np.arange(N_BUCKETS), F32)
    bias = jnp.einsum("qsb,bh->qsh", onehot, rpb_table.astype(F32),
                      precision=lax.Precision.HIGHEST)
    bias = jnp.where(band[:, :, None], bias, NEG_INF)
    bias = jnp.transpose(bias, (2, 0, 1))
    sink = jnp.full((B_HEADS, WIN_ROWS, LANES), NEG_INF, F32)
    sink = sink.at[:, :, 0].set(jnp.broadcast_to(b_sink_l.astype(F32)[:, None], (B_HEADS, WIN_ROWS)))
    ext = jnp.concatenate([bias, sink], axis=-1)
    group = B_HEADS // B_KV_HEADS
    return ext.reshape(B_KV_HEADS, group * WIN_ROWS, span + LANES)


def _pad_cols(w, groups, width, total):
    k = w.shape[0]
    w = w.reshape(k, groups, width)
    w = jnp.pad(w, ((0, 0), (0, 0), (0, total - width)))
    return w.reshape(k, groups * total)


def _layer_params(l, w_in, a_q_norm_g, a_k_norm_g, c_q_norm_g, c_kv_norm_g, c_w_uq, c_w_ukv,
                  w_branch_a, w_branch_b, w_branch_c, w_o, w_gate, w_up, w_down):
    w = w_in[l]
    splits = np.cumsum([0, A_HEADS * HEAD_DIM, A_KV_HEADS * HEAD_DIM, A_KV_HEADS * HEAD_DIM,
                        B_HEADS * HEAD_DIM, B_KV_HEADS * HEAD_DIM, B_KV_HEADS * HEAD_DIM,
                        C_Q_LORA, C_KV_LORA, C_ROPE, D_MODEL, D_MODEL, D_MODEL])
    part = [w[:, splits[i]:splits[i + 1]] for i in range(12)]
    qa, ka, va, qb, kb, vb, cq, ckv, kr, ga, gb, gc = part
    va_e = _pad_cols(va, A_KV_HEADS, HEAD_DIM, LANES)
    kr_p = jnp.pad(kr, ((0, 0), (C_NOPE, LANES - C_NOPE - C_ROPE)))
    w1 = jnp.concatenate([qa, ka, va_e, qb * (HEAD_DIM ** -0.5), kb, vb, cq, ckv, kr_p], axis=1).astype(BF16)
    gate_a = jnp.concatenate([jnp.tile(a_q_norm_g[l], A_HEADS) * (HEAD_DIM ** -0.5 * LOG2E),
                              jnp.tile(a_k_norm_g[l], A_KV_HEADS)])[None, :].astype(F32)
    wuq = _pad_cols(c_w_uq[l], C_HEADS, C_NOPE + C_ROPE, LANES).astype(BF16)
    ukv = c_w_ukv[l].reshape(C_KV_LORA, C_HEADS, C_NOPE + C_V)
    wuk = jnp.pad(ukv[:, :, :C_NOPE], ((0, 0), (0, 0), (0, LANES - C_NOPE))).reshape(C_KV_LORA, C_HEADS * LANES)
    wuv = jnp.pad(ukv[:, :, C_NOPE:], ((0, 0), (0, 0), (0, LANES - C_V))).reshape(C_KV_LORA, C_HEADS * LANES)
    wukv = jnp.concatenate([wuk, wuv], axis=1).astype(BF16)
    wg = jnp.concatenate([ga, gb, gc], axis=1).astype(BF16)
    wgu = jnp.concatenate([w_gate[l], w_up[l]], axis=-1).astype(BF16)
    return dict(w1=w1, gate_a=gate_a, gcq=c_q_norm_g[l][None, :].astype(F32),
                gckv=c_kv_norm_g[l][None, :].astype(F32), wuq=wuq, wukv=wukv, wg=wg,
                wa=w_branch_a[l].astype(BF16), wb=w_branch_b[l].astype(BF16),
                wc=w_branch_c[l].astype(BF16), wo=w_o[l].astype(BF16),
                wgu=wgu, wd=w_down[l].astype(BF16))


def kernel(x, ln_in_g, ln_in_b, w_in, a_q_norm_g, a_k_norm_g, b_sink, rpb_table, c_q_norm_g, c_kv_norm_g,
           c_w_uq, c_w_ukv, w_branch_a, w_branch_b, w_branch_c, w_o, ln1_g, ln1_b, router_w, router_bias,
           w_gate, w_up, w_down, ln2_g, ln2_b):
    batch, seq, d = x.shape
    depth = w_in.shape[0]
    n = batch * seq
    alpha = (2 * depth) ** 0.25
    assert d == D_MODEL and seq % PROJ_ROWS == 0 and seq % ATTN_Q_ROWS == 0 and seq % WIN_ROWS == 0
    assert n % DISPATCH_ROWS == 0 and n % MERGE_ROWS == 0 and n % COMBINE_ROWS == 0

    cos_a, sin_a, cos_q, sin_q, cos_k, sin_k = _rope_tables(seq)
    seg = np.arange(PA_W) // HEAD_DIM
    bd = jnp.asarray((seg[:, None] == seg[None, :]).astype(np.float32) / HEAD_DIM, BF16)
    ones_pat = np.zeros((1, LANES), np.float32)
    ones_pat[0, HEAD_DIM:] = 1.0
    vones_a = jnp.asarray(np.tile(ones_pat, (1, A_KV_HEADS)))
    vones_c = jnp.asarray(np.tile(ones_pat, (1, C_HEADS)))
    tri = jnp.asarray(np.triu(np.ones((MERGE_ROWS, MERGE_ROWS), np.float32), k=1), BF16)
    rwt = router_w.astype(F32).T
    rb = router_bias.astype(F32)[:, None]
    lng = ln_in_g.astype(F32)[None, :]
    lnb = ln_in_b.astype(F32)[None, :]

    tme = EXPERT_ROWS
    n_tiles = (TOP_K * n) // tme + N_EXPERTS
    p_rows = n_tiles * tme

    h = x.reshape(n, d).astype(F32)
    for l in range(depth):
        prm = _layer_params(l, w_in, a_q_norm_g, a_k_norm_g, c_q_norm_g, c_kv_norm_g, c_w_uq, c_w_ukv,
                            w_branch_a, w_branch_b, w_branch_c, w_o, w_gate, w_up, w_down)
        outs = _proj_call(h, lng, lnb, prm["w1"], bd, prm["gate_a"], cos_a, sin_a, prm["gcq"], prm["gckv"],
                          prm["wuq"], prm["wukv"], cos_q, sin_q, cos_k, sin_k, vones_a, vones_c,
                          seq=seq, apply_ln=(l == 0))
        if l == 0:
            h, *outs = outs
        qa, ka, va, qb, kb, vb, qc, kc, vc = outs
        oa = _dense_attn_call(qa, ka, va, batch=batch, seq=seq, heads=A_HEADS, kv_heads=A_KV_HEADS,
                              dq=HEAD_DIM, name="attn_a")
        bias = _window_bias(rpb_table, b_sink[l])
        ob = _win_attn_call(qb, kb, vb, bias, batch=batch, seq=seq, heads=B_HEADS, kv_heads=B_KV_HEADS)
        oc = _dense_attn_call(qc, kc, vc, batch=batch, seq=seq, heads=C_HEADS, kv_heads=C_HEADS,
                              dq=LANES, name="attn_c")
        x1, xp, e_idx, wts, rank, cnt = _merge_call(
            h, oa, ob, oc, prm["wg"], prm["wa"], prm["wb"], prm["wc"], prm["wo"],
            ln1_g[l].astype(F32)[None, :], ln1_b[l].astype(F32)[None, :], rwt, rb, tri, alpha=alpha)

        counts = cnt[:, 0].astype(jnp.int32)
        padded = ((counts + tme - 1) // tme) * tme
        ends = jnp.cumsum(padded)
        offs = ends - padded
        pos = rank
        for e in range(N_EXPERTS):
            pos = pos + jnp.where(e_idx == e, offs[e], 0)
        tile_start = jnp.arange(n_tiles, dtype=jnp.int32) * tme
        tile_expert = jnp.minimum(jnp.sum(tile_start[:, None] >= ends[None, :], axis=1), N_EXPERTS - 1)
        tile_expert = tile_expert.astype(jnp.int32)
        n_used = (ends[-1:] // tme).astype(jnp.int32)

        def per_step(rows):
            return pos.reshape(TOP_K, n // rows, rows).transpose(1, 0, 2).reshape(n // rows, 1, TOP_K * rows)

        xs = _dispatch_call(per_step(DISPATCH_ROWS), xp, jnp.zeros((p_rows, ROW_TILES, LANES), jnp.uint32))
        ys = _expert_call(tile_expert, n_used, xs, prm["wgu"], prm["wd"])
        pos_c = per_step(COMBINE_ROWS)
        h = _combine_call(pos_c, ys, x1, wts.T, ln2_g[l].astype(F32)[None, :], ln2_b[l].astype(F32)[None, :],
                          alpha=alpha)
    return h.reshape(batch, seq, d).astype(x.dtype)
```

```python
import functools
import math

import numpy as np
import jax
import jax.numpy as jnp
from jax import lax
from jax.experimental import pallas as pl
from jax.experimental.pallas import tpu as pltpu

F32 = jnp.float32
BF16 = jnp.bfloat16

D_MODEL = 1024
GRID_W = 64
HEAD_DIM = 64
A_HEADS, A_KV_HEADS = 8, 2
B_HEADS, B_KV_HEADS = 8, 2
WINDOW = 128
C_HEADS = 8
C_NOPE, C_ROPE, C_V = 64, 32, 64
C_Q_LORA, C_KV_LORA = 384, 256
N_BUCKETS, MAX_DISTANCE = 32, 128
N_EXPERTS, N_GROUPS, TOP_K = 16, 4, 2
EXPERTS_PER_GROUP = N_EXPERTS // N_GROUPS
D_FF = 512
ROPE_THETA = 10000.0
NORM_EPS = 1e-6
NEG_INF = -1e30

LANES = 128
ROW_TILES = D_MODEL // 2 // LANES
BF16_SUBLANES = 16
VT_ROWS = HEAD_DIM + BF16_SUBLANES
LOG2E = math.log2(math.e)
VMEM_LIMIT_BYTES = 48 * 1024 * 1024

PROJ_ROWS = 512
ATTN_Q_ROWS = 512
WIN_ROWS = 128
WIN_BLOCKS_PER_STEP = 2
MERGE_ROWS = 512
DISPATCH_ROWS = 1024
EXPERT_ROWS = 512
COMBINE_ROWS = 256
DMA_UNROLL = 8

PA_W = A_HEADS * HEAD_DIM + A_KV_HEADS * HEAD_DIM
PA_V = A_KV_HEADS * LANES
PB_W = B_HEADS * HEAD_DIM + 2 * B_KV_HEADS * HEAD_DIM
PC_W = C_Q_LORA + C_KV_LORA + LANES
OFF_A, OFF_AV = 0, PA_W
OFF_B = OFF_AV + PA_V
OFF_C = OFF_B + PB_W
P1_COLS = OFF_C + PC_W


def _cparams(semantics):
    return pltpu.CompilerParams(dimension_semantics=semantics, vmem_limit_bytes=VMEM_LIMIT_BYTES)


def _layer_norm(x, g, b):
    mu = jnp.mean(x, axis=-1, keepdims=True)
    xc = x - mu
    var = jnp.mean(xc * xc, axis=-1, keepdims=True)
    return xc * lax.rsqrt(var + NORM_EPS) * g + b


def _swap16(x):
    lane = lax.broadcasted_iota(jnp.int32, x.shape, x.ndim - 1)
    up = pltpu.roll(x, LANES - 16, axis=x.ndim - 1)
    down = pltpu.roll(x, 16, axis=x.ndim - 1)
    return jnp.where((lane & 16) == 0, up, down)


def _rope_tiles(x, cos, sin):
    tiles = []
    for c in range(x.shape[-1] // LANES):
        xc = x[:, c * LANES:(c + 1) * LANES]
        tiles.append(xc * cos + _swap16(xc) * sin)
    return jnp.concatenate(tiles, axis=-1) if len(tiles) > 1 else tiles[0]


def _store_vt(vt_ref, v_ext, kv_heads):
    vt = v_ext.T
    for kv in range(kv_heads):
        vt_ref[kv * VT_ROWS:(kv + 1) * VT_ROWS, :] = vt[kv * LANES:kv * LANES + VT_ROWS].astype(BF16)


def _proj_kernel(x_ref, lng_ref, lnb_ref, w1_ref, bd_ref, ga_ref, cosa_ref, sina_ref,
                 gcq_ref, gckv_ref, wuq_ref, wukv_ref, cosq_ref, sinq_ref, cosk_ref, sink_ref,
                 vones_a_ref, vones_c_ref,
                 *out_refs, apply_ln):
    if apply_ln:
        xn_ref, qa_ref, ka_ref, va_ref, qb_ref, kb_ref, vb_ref, qc_ref, kc_ref, vc_ref = out_refs
    else:
        qa_ref, ka_ref, va_ref, qb_ref, kb_ref, vb_ref, qc_ref, kc_ref, vc_ref = out_refs
    x = x_ref[...]
    if apply_ln:
        x = _layer_norm(x, lng_ref[...], lnb_ref[...])
        xn_ref[...] = x
    xb = x.astype(BF16)
    p = jnp.dot(xb, w1_ref[...], preferred_element_type=F32)

    qk = p[:, OFF_A:OFF_A + PA_W]
    ms = jnp.dot((qk * qk).astype(BF16), bd_ref[...], preferred_element_type=F32)
    qk = qk * lax.rsqrt(ms + NORM_EPS) * ga_ref[...]
    qk = _rope_tiles(qk, cosa_ref[...], sina_ref[...])
    nq = A_HEADS * HEAD_DIM
    qa_ref[...] = qk[:, :nq].astype(BF16)
    kk = qk[:, nq:]
    low = lax.broadcasted_iota(jnp.int32, kk.shape, 1) < HEAD_DIM
    ka_ref[...] = jnp.concatenate([jnp.where(low, kk, 0.0),
                                   jnp.where(low, pltpu.roll(kk, HEAD_DIM, axis=1), 0.0)],
                                  axis=-1).astype(BF16)
    _store_vt(va_ref, p[:, OFF_AV:OFF_AV + PA_V] + vones_a_ref[...], A_KV_HEADS)

    nqb = B_HEADS * HEAD_DIM
    nkb = B_KV_HEADS * HEAD_DIM
    qb_ref[...] = p[:, OFF_B:OFF_B + nqb].astype(BF16)
    kb_ref[...] = p[:, OFF_B + nqb:OFF_B + nqb + nkb].astype(BF16)
    vb_ref[...] = p[:, OFF_B + nqb + nkb:OFF_B + PB_W].astype(BF16)

    cq = p[:, OFF_C:OFF_C + C_Q_LORA]
    ckv = p[:, OFF_C + C_Q_LORA:OFF_C + C_Q_LORA + C_KV_LORA]
    kr = p[:, OFF_C + C_Q_LORA + C_KV_LORA:OFF_C + PC_W]
    cqn = cq * lax.rsqrt(jnp.mean(cq * cq, axis=-1, keepdims=True) + NORM_EPS) * gcq_ref[...]
    ckvn = ckv * lax.rsqrt(jnp.mean(ckv * ckv, axis=-1, keepdims=True) + NORM_EPS) * gckv_ref[...]
    qc = jnp.dot(cqn.astype(BF16), wuq_ref[...], preferred_element_type=F32)
    qc_ref[...] = _rope_tiles(qc, cosq_ref[...], sinq_ref[...]).astype(BF16)
    kv = jnp.dot(ckvn.astype(BF16), wukv_ref[...], preferred_element_type=F32)
    kpe = _rope_tiles(kr, cosk_ref[...], sink_ref[...])
    hw = C_HEADS * LANES
    kc_ref[...] = (kv[:, :hw] + jnp.concatenate([kpe] * C_HEADS, axis=-1)).astype(BF16)
    _store_vt(vc_ref, kv[:, hw:] + vones_c_ref[...], C_HEADS)


def _proj_call(x, lng, lnb, w1, bd, ga, cosa, sina, gcq, gckv, wuq, wukv, cosq, sinq, cosk, sink,
               vones_a, vones_c, *, seq, apply_ln):
    n = x.shape[0]
    tm = PROJ_ROWS
    nsb = seq // tm
    row = lambda w: pl.BlockSpec((tm, w), lambda i: (i, 0))
    full = lambda a: pl.BlockSpec(a.shape, lambda i: (0,) * a.ndim)
    tab = pl.BlockSpec((tm, LANES), lambda i: (i % nsb, 0))
    out_w = [A_HEADS * HEAD_DIM, A_KV_HEADS * LANES, -A_KV_HEADS * VT_ROWS,
             B_HEADS * HEAD_DIM, B_KV_HEADS * HEAD_DIM, B_KV_HEADS * HEAD_DIM,
             C_HEADS * LANES, C_HEADS * LANES, -C_HEADS * VT_ROWS]
    out_shape = [jax.ShapeDtypeStruct((n, w) if w > 0 else (-w, n), BF16) for w in out_w]
    out_specs = [row(w) if w > 0 else pl.BlockSpec((-w, tm), lambda i: (0, i)) for w in out_w]
    if apply_ln:
        out_shape = [jax.ShapeDtypeStruct((n, D_MODEL), F32)] + out_shape
        out_specs = [row(D_MODEL)] + out_specs
    return pl.pallas_call(
        functools.partial(_proj_kernel, apply_ln=apply_ln),
        out_shape=out_shape,
        grid=(n // tm,),
        in_specs=[row(D_MODEL), full(lng), full(lnb), full(w1), full(bd), full(ga), tab, tab,
                  full(gcq), full(gckv), full(wuq), full(wukv), tab, tab, tab, tab,
                  full(vones_a), full(vones_c)],
        out_specs=out_specs,
        compiler_params=_cparams(("parallel",)),
        name="proj_ln" if apply_ln else "proj",
    )(x, lng, lnb, w1, bd, ga, cosa, sina, gcq, gckv, wuq, wukv, cosq, sinq, cosk, sink,
      vones_a, vones_c)


def _dense_attn_kernel(q_ref, k_ref, vt_ref, o_ref, *, heads, kv_heads, dq):
    group = heads // kv_heads

    def scores_t(h):
        kv = h // group
        qh = q_ref[:, h * dq:(h + 1) * dq]
        kh = k_ref[:, kv * LANES:(kv + 1) * LANES]
        if dq < LANES:
            kh = kh[:, :dq]
        return lax.dot_general(kh, qh, (((1,), (1,)), ((), ())), preferred_element_type=F32)

    outs = []
    st = {h: scores_t(h) for h in range(min(2, heads))}
    for h in range(heads):
        kv = h // group
        s = st.pop(h)
        m = jnp.max(s, axis=0, keepdims=True)
        pt = jnp.exp2(s - m).astype(BF16)
        if h + 2 < heads:
            st[h + 2] = scores_t(h + 2)
        vth = vt_ref[kv * VT_ROWS:(kv + 1) * VT_ROWS, :]
        ot = jnp.dot(vth, pt, preferred_element_type=F32)
        outs.append(ot[:HEAD_DIM] / ot[HEAD_DIM:HEAD_DIM + 1])
    o_ref[...] = jnp.concatenate(outs, axis=0).T.astype(o_ref.dtype)


def _dense_attn_call(q, k, vt, *, batch, seq, heads, kv_heads, dq, name):
    n = q.shape[0]
    tq = ATTN_Q_ROWS
    nq = seq // tq
    return pl.pallas_call(
        functools.partial(_dense_attn_kernel, heads=heads, kv_heads=kv_heads, dq=dq),
        out_shape=jax.ShapeDtypeStruct((n, heads * HEAD_DIM), BF16),
        grid=(batch, nq),
        in_specs=[pl.BlockSpec((tq, heads * dq), lambda b, i: (b * nq + i, 0)),
                  pl.BlockSpec((seq, kv_heads * LANES), lambda b, i: (b, 0)),
                  pl.BlockSpec((kv_heads * VT_ROWS, seq), lambda b, i: (0, b))],
        out_specs=pl.BlockSpec((tq, heads * HEAD_DIM), lambda b, i: (b * nq + i, 0)),
        compiler_params=_cparams(("parallel", "parallel")),
        name=name,
    )(q, k, vt)


def _win_attn_kernel(q_ref, k_ref, v_ref, bias_ref, o_ref, *, heads, kv_heads, nb):
    group = heads // kv_heads
    tq = WIN_ROWS
    for j in range(WIN_BLOCKS_PER_STEP):
        i = pl.program_id(1) * WIN_BLOCKS_PER_STEP + j
        outs = [None] * heads
        for kv in range(kv_heads):
            qs = jnp.concatenate(
                [q_ref[j * tq:(j + 1) * tq, (kv * group + g) * HEAD_DIM:(kv * group + g + 1) * HEAD_DIM]
                 for g in range(group)], axis=0)
            bias = bias_ref[kv]
            logit_chunks, v_chunks = [], []
            for c in range(3):
                blk = i + (c - 1)
                start = pl.multiple_of(jnp.clip(blk, 0, nb - 1) * tq, tq)
                kc = k_ref[pl.ds(start, tq), kv * HEAD_DIM:(kv + 1) * HEAD_DIM]
                v_chunks.append(v_ref[pl.ds(start, tq), kv * HEAD_DIM:(kv + 1) * HEAD_DIM])
                sc = lax.dot_general(qs, kc, (((1,), (1,)), ((), ())), preferred_element_type=F32)
                sc = sc + bias[:, c * tq:(c + 1) * tq]
                in_range = jnp.logical_and(blk >= 0, blk <= nb - 1)
                logit_chunks.append(jnp.where(in_range, sc, NEG_INF))
            logit_chunks.append(bias[:, 3 * tq:])
            logits = jnp.concatenate(logit_chunks, axis=-1)
            m = jnp.max(logits, axis=-1, keepdims=True)
            pr = jnp.exp2(logits - m)
            denom = jnp.sum(pr, axis=-1, keepdims=True)
            vv = jnp.concatenate(v_chunks, axis=0)
            o = jnp.dot(pr[:, :3 * tq].astype(BF16), vv, preferred_element_type=F32) / denom
            for g in range(group):
                outs[kv * group + g] = o[g * tq:(g + 1) * tq]
        o_ref[j * tq:(j + 1) * tq, :] = jnp.concatenate(outs, axis=-1).astype(o_ref.dtype)


def _win_attn_call(q, k, v, bias, *, batch, seq, heads, kv_heads):
    n = q.shape[0]
    ts = WIN_ROWS * WIN_BLOCKS_PER_STEP
    nq = seq // ts
    return pl.pallas_call(
        functools.partial(_win_attn_kernel, heads=heads, kv_heads=kv_heads, nb=seq // WIN_ROWS),
        out_shape=jax.ShapeDtypeStruct((n, heads * HEAD_DIM), BF16),
        grid=(batch, nq),
        in_specs=[pl.BlockSpec((ts, heads * HEAD_DIM), lambda b, i: (b * nq + i, 0)),
                  pl.BlockSpec((seq, kv_heads * HEAD_DIM), lambda b, i: (b, 0)),
                  pl.BlockSpec((seq, kv_heads * HEAD_DIM), lambda b, i: (b, 0)),
                  pl.BlockSpec(bias.shape, lambda b, i: (0, 0, 0))],
        out_specs=pl.BlockSpec((ts, heads * HEAD_DIM), lambda b, i: (b * nq + i, 0)),
        compiler_params=_cparams(("parallel", "parallel")),
        name="win_attn",
    )(q, k, v, bias)


def _pack_rows(x):
    half = x.shape[-1] // 2
    hi = pltpu.bitcast(x[:, :half].astype(BF16).astype(F32), jnp.uint32)
    lo = pltpu.bitcast(x[:, half:].astype(BF16).astype(F32), jnp.uint32)
    return hi | (lo >> 16)


def _unpack_rows(u):
    hi = pltpu.bitcast(u & jnp.uint32(0xFFFF0000), F32)
    lo = pltpu.bitcast(u << 16, F32)
    return jnp.concatenate([hi, lo], axis=-1)


def _store_row_tiles(ref, u):
    for c in range(ROW_TILES):
        ref[:, c, :] = u[:, c * LANES:(c + 1) * LANES]


def _load_row_tiles(ref):
    return jnp.concatenate([ref[:, c, :] for c in range(ROW_TILES)], axis=-1)


def _merge_kernel(x_ref, oa_ref, ob_ref, oc_ref, wg_ref, wa_ref, wb_ref, wc_ref, wo_ref,
                  g_ref, b_ref, rwt_ref, rb_ref, tri_ref,
                  x1_ref, xp_ref, e_ref, w_ref, r_ref, cnt_ref, carry_ref, *, alpha):
    step = pl.program_id(0)

    @pl.when(step == 0)
    def _():
        carry_ref[...] = jnp.zeros_like(carry_ref)

    x = x_ref[...]
    xb = x.astype(BF16)
    d = D_MODEL
    merged = None
    for idx, (o_ref, wbr_ref) in enumerate(((oa_ref, wa_ref), (ob_ref, wb_ref), (oc_ref, wc_ref))):
        gate = jax.nn.sigmoid(jnp.dot(xb, wg_ref[:, idx * d:(idx + 1) * d], preferred_element_type=F32))
        br = jnp.dot(o_ref[...], wbr_ref[...], preferred_element_type=F32)
        merged = gate * br if merged is None else merged + gate * br
    mix = jnp.dot(merged.astype(BF16), wo_ref[...], preferred_element_type=F32)
    x1 = _layer_norm(alpha * x + mix, g_ref[...], b_ref[...])
    x1_ref[...] = x1
    _store_row_tiles(xp_ref, _pack_rows(x1))

    logits = lax.dot_general(rwt_ref[...], x1, (((1,), (1,)), ((), ())),
                             preferred_element_type=F32, precision=lax.Precision.HIGHEST)
    scores = jax.nn.sigmoid(logits)
    biased = scores + rb_ref[...]
    shape = biased.shape
    rows = lax.broadcasted_iota(jnp.int32, shape, 0)
    sub = lax.broadcasted_iota(jnp.int32, (EXPERTS_PER_GROUP, shape[1]), 0)
    best, sel = None, None
    for g in range(N_GROUPS):
        blk = biased[g * EXPERTS_PER_GROUP:(g + 1) * EXPERTS_PER_GROUP]
        m1 = jnp.max(blk, axis=0, keepdims=True)
        i1 = jnp.min(jnp.where(blk == m1, sub, EXPERTS_PER_GROUP), axis=0, keepdims=True)
        m2 = jnp.max(jnp.where(sub == i1, -jnp.inf, blk), axis=0, keepdims=True)
        gs = m1 + m2
        if best is None:
            best, sel = gs, jnp.zeros_like(i1)
        else:
            better = gs > best
            sel = jnp.where(better, g, sel)
            best = jnp.where(better, gs, best)
    in_group = (rows >> 2) == sel
    bm = jnp.where(in_group, biased, -jnp.inf)
    t1 = jnp.max(bm, axis=0, keepdims=True)
    e0 = jnp.min(jnp.where(bm == t1, rows, N_EXPERTS), axis=0, keepdims=True)
    bm2 = jnp.where(rows == e0, -jnp.inf, bm)
    t2 = jnp.max(bm2, axis=0, keepdims=True)
    e1 = jnp.min(jnp.where(bm2 == t2, rows, N_EXPERTS), axis=0, keepdims=True)
    hot0 = rows == e0
    hot1 = rows == e1
    s0 = jnp.sum(jnp.where(hot0, scores, 0.0), axis=0, keepdims=True)
    s1 = jnp.sum(jnp.where(hot1, scores, 0.0), axis=0, keepdims=True)
    tot = s0 + s1
    e_ref[...] = jnp.concatenate([e0, e1], axis=0)
    w_ref[...] = jnp.concatenate([s0 / tot, s1 / tot], axis=0)

    hot = jnp.where(jnp.logical_or(hot0, hot1), 1.0, 0.0)
    prefix = jnp.dot(hot.astype(BF16), tri_ref[...], preferred_element_type=F32) + carry_ref[:, 0:1]
    r0 = jnp.sum(jnp.where(hot0, prefix, 0.0), axis=0, keepdims=True)
    r1 = jnp.sum(jnp.where(hot1, prefix, 0.0), axis=0, keepdims=True)
    r_ref[...] = jnp.concatenate([r0, r1], axis=0).astype(jnp.int32)
    carry_ref[...] = carry_ref[...] + jnp.sum(hot, axis=1, keepdims=True)
    cnt_ref[...] = carry_ref[...]


def _merge_call(x, oa, ob, oc, wg, wa, wb, wc, wo, g, b, rwt, rb, tri, *, alpha):
    n = x.shape[0]
    tm = MERGE_ROWS
    row = lambda w: pl.BlockSpec((tm, w), lambda i: (i, 0))
    full = lambda a: pl.BlockSpec(a.shape, lambda i: (0,) * a.ndim)
    tok = pl.BlockSpec((TOP_K, tm), lambda i: (0, i))
    packed = pl.BlockSpec((tm, ROW_TILES, LANES), lambda i: (i, 0, 0))
    return pl.pallas_call(
        functools.partial(_merge_kernel, alpha=alpha),
        out_shape=[jax.ShapeDtypeStruct((n, D_MODEL), F32),
                   jax.ShapeDtypeStruct((n, ROW_TILES, LANES), jnp.uint32),
                   jax.ShapeDtypeStruct((TOP_K, n), jnp.int32),
                   jax.ShapeDtypeStruct((TOP_K, n), F32),
                   jax.ShapeDtypeStruct((TOP_K, n), jnp.int32),
                   jax.ShapeDtypeStruct((N_EXPERTS, LANES), F32)],
        grid=(n // tm,),
        in_specs=[row(D_MODEL), row(oa.shape[1]), row(ob.shape[1]), row(oc.shape[1]),
                  full(wg), full(wa), full(wb), full(wc), full(wo), full(g), full(b),
                  full(rwt), full(rb), full(tri)],
        out_specs=[row(D_MODEL), packed, tok, tok, tok,
                   pl.BlockSpec((N_EXPERTS, LANES), lambda i: (0, 0))],
        scratch_shapes=[pltpu.VMEM((N_EXPERTS, LANES), F32)],
        compiler_params=_cparams(("arbitrary",)),
        name="merge",
    )(x, oa, ob, oc, wg, wa, wb, wc, wo, g, b, rwt, rb, tri)


def _dispatch_kernel(pos_ref, x_ref, dst_in_hbm, dst_hbm, sem):
    del dst_in_hbm
    tb = DISPATCH_ROWS

    def start(g, carry):
        for u in range(DMA_UNROLL):
            t = g * DMA_UNROLL + u
            for j in range(TOP_K):
                pltpu.make_async_copy(x_ref.at[t], dst_hbm.at[pos_ref[0, 0, j * tb + t]],
                                      sem).start(priority=(u + j) % 2)
        return carry

    lax.fori_loop(0, tb // DMA_UNROLL, start, 0)
    for j in range(TOP_K):
        pltpu.make_async_copy(x_ref, dst_hbm.at[pl.ds(0, tb)], sem).wait()


def _dispatch_call(pos_flat, xp, dst0):
    nsteps = pos_flat.shape[0]
    tb = DISPATCH_ROWS
    return pl.pallas_call(
        _dispatch_kernel,
        out_shape=jax.ShapeDtypeStruct(dst0.shape, dst0.dtype),
        grid=(nsteps,),
        in_specs=[pl.BlockSpec((1, 1, TOP_K * tb), lambda i: (i, 0, 0), memory_space=pltpu.SMEM),
                  pl.BlockSpec((tb, ROW_TILES, LANES), lambda i: (i, 0, 0)),
                  pl.BlockSpec(memory_space=pl.ANY)],
        out_specs=pl.BlockSpec(memory_space=pl.ANY),
        scratch_shapes=[pltpu.SemaphoreType.DMA(())],
        input_output_aliases={2: 0},
        compiler_params=pltpu.CompilerParams(dimension_semantics=("arbitrary",), has_side_effects=True,
                                             vmem_limit_bytes=VMEM_LIMIT_BYTES),
        name="dispatch",
    )(pos_flat, xp, dst0)


def _expert_kernel(te_ref, nu_ref, xs_ref, wg_ref, wu_ref, wd_ref, ys_ref, wgu_bf, wd_bf):
    i = pl.program_id(0)

    @pl.when(jnp.logical_or(i == 0, te_ref[i] != te_ref[jnp.maximum(i - 1, 0)]))
    def _():
        wgu_bf[:, :D_FF] = wg_ref[0, 0].astype(BF16)
        wgu_bf[:, D_FF:] = wu_ref[0, 0].astype(BF16)
        wd_bf[...] = wd_ref[0, 0].astype(BF16)

    @pl.when(i < nu_ref[0])
    def _():
        xb = _unpack_rows(_load_row_tiles(xs_ref)).astype(BF16)
        gu = jnp.dot(xb, wgu_bf[...], preferred_element_type=F32)
        gate, up = gu[:, :D_FF], gu[:, D_FF:]
        hid = (gate * jax.nn.sigmoid(gate)) * up
        y = jnp.dot(hid.astype(BF16), wd_bf[...], preferred_element_type=F32)
        _store_row_tiles(ys_ref, _pack_rows(y))

    @pl.when(i >= nu_ref[0])
    def _():
        ys_ref[...] = jnp.zeros_like(ys_ref)


def _expert_call(tile_expert, n_used, xs, w_gate, w_up, w_down, *, layer):
    p = xs.shape[0]
    tm = EXPERT_ROWS
    wspec = lambda r, c: pl.BlockSpec((1, 1, r, c), lambda i, te, nu: (layer, te[i], 0, 0))
    grid_spec = pltpu.PrefetchScalarGridSpec(
        num_scalar_prefetch=2,
        grid=(p // tm,),
        in_specs=[pl.BlockSpec((tm, ROW_TILES, LANES), lambda i, te, nu: (i, 0, 0)),
                  wspec(D_MODEL, D_FF), wspec(D_MODEL, D_FF), wspec(D_FF, D_MODEL)],
        out_specs=pl.BlockSpec((tm, ROW_TILES, LANES), lambda i, te, nu: (i, 0, 0)),
        scratch_shapes=[pltpu.VMEM((D_MODEL, 2 * D_FF), BF16), pltpu.VMEM((D_FF, D_MODEL), BF16)],
    )
    return pl.pallas_call(
        _expert_kernel,
        out_shape=jax.ShapeDtypeStruct((p, ROW_TILES, LANES), jnp.uint32),
        grid_spec=grid_spec,
        compiler_params=_cparams(("arbitrary",)),
        name="experts",
    )(tile_expert, n_used, xs, w_gate, w_up, w_down)


def _combine_kernel(pos_ref, nxt_ref, ys_hbm, x_ref, w_ref, g_ref, b_ref, o_ref, buf_ref, sem, *, alpha):
    tc = COMBINE_ROWS
    i = pl.program_id(0)
    slot = i % 2

    def gather(idx_ref, to_slot):
        def start(g, carry):
            for u in range(DMA_UNROLL):
                t = g * DMA_UNROLL + u
                for j in range(TOP_K):
                    pltpu.make_async_copy(ys_hbm.at[idx_ref[0, 0, j * tc + t]], buf_ref.at[to_slot, j, t],
                                          sem.at[to_slot]).start(priority=(u + j) % 2)
            return carry
        lax.fori_loop(0, tc // DMA_UNROLL, start, 0)

    @pl.when(i == 0)
    def _():
        gather(pos_ref, 0)

    @pl.when(i + 1 < pl.num_programs(0))
    def _():
        gather(nxt_ref, 1 - slot)

    for j in range(TOP_K):
        pltpu.make_async_copy(ys_hbm.at[pl.ds(0, tc)], buf_ref.at[slot, j], sem.at[slot]).wait()
    w = w_ref[...]
    ffn = (w[:, 0:1] * _unpack_rows(_load_row_tiles(buf_ref.at[slot, 0]))
           + w[:, 1:2] * _unpack_rows(_load_row_tiles(buf_ref.at[slot, 1])))
    o_ref[...] = _layer_norm(alpha * x_ref[...] + ffn, g_ref[...], b_ref[...])


def _combine_call(pos_flat, ys, x1, wts_t, g, b, *, alpha):
    n = x1.shape[0]
    tc = COMBINE_ROWS
    nsteps = n // tc
    idx = lambda shift: pl.BlockSpec((1, 1, TOP_K * tc), lambda i: (jnp.minimum(i + shift, nsteps - 1), 0, 0),
                                     memory_space=pltpu.SMEM)
    return pl.pallas_call(
        functools.partial(_combine_kernel, alpha=alpha),
        out_shape=jax.ShapeDtypeStruct((n, D_MODEL), F32),
        grid=(nsteps,),
        in_specs=[idx(0), idx(1),
                  pl.BlockSpec(memory_space=pl.ANY),
                  pl.BlockSpec((tc, D_MODEL), lambda i: (i, 0)),
                  pl.BlockSpec((tc, TOP_K), lambda i: (i, 0)),
                  pl.BlockSpec(g.shape, lambda i: (0, 0)),
                  pl.BlockSpec(b.shape, lambda i: (0, 0))],
        out_specs=pl.BlockSpec((tc, D_MODEL), lambda i: (i, 0)),
        scratch_shapes=[pltpu.VMEM((2, TOP_K, tc, ROW_TILES, LANES), jnp.uint32),
                        pltpu.SemaphoreType.DMA((2,))],
        compiler_params=_cparams(("arbitrary",)),
        name="combine",
    )(pos_flat, pos_flat, ys, x1, wts_t, g, b)


def _t5_bucket(rel):
    nb = N_BUCKETS // 2
    ret = (rel > 0).astype(np.int32) * nb
    n = np.abs(rel)
    max_exact = nb // 2
    large = max_exact + (np.log(np.maximum(n, 1) / max_exact)
                         / math.log(MAX_DISTANCE / max_exact) * (nb - max_exact)).astype(np.int32)
    large = np.minimum(large, nb - 1)
    return (ret + np.where(n < max_exact, n, large)).astype(np.int32)


def _rope_tables(seq):
    def cs(pos, dim):
        inv = ROPE_THETA ** (-(jnp.arange(0, dim, 2, dtype=F32) / dim))
        ang = pos.astype(F32)[:, None] * inv[None, :]
        return jnp.cos(ang), jnp.sin(ang)

    t = jnp.arange(seq)
    cr, sr = cs(t // GRID_W, HEAD_DIM // 2)
    cc, sc = cs(t % GRID_W, HEAD_DIM // 2)
    cos_head = jnp.concatenate([cr, cr, cc, cc], axis=-1)
    sin_head = jnp.concatenate([-sr, sr, -sc, sc], axis=-1)
    cos_a = jnp.concatenate([cos_head, cos_head], axis=-1)
    sin_a = jnp.concatenate([sin_head, sin_head], axis=-1)
    cs_, ss_ = cs(t, C_ROPE)
    ones = jnp.ones((seq, C_NOPE), F32)
    pad1 = jnp.ones((seq, LANES - C_NOPE - C_ROPE), F32)
    cos_k = jnp.concatenate([ones, cs_, cs_, pad1], axis=-1)
    sin_k = jnp.concatenate([0 * ones, -ss_, ss_, 0 * pad1], axis=-1)
    scale = (C_NOPE + C_ROPE) ** -0.5 * LOG2E
    return cos_a, sin_a, cos_k * scale, sin_k * scale, cos_k, sin_k


def _window_bias(rpb_table, b_sink_l):
    span = WIN_ROWS + 2 * WINDOW
    rel = np.arange(span)[None, :] - WINDOW - np.arange(WIN_ROWS)[:, None]
    band = jnp.asarray(np.abs(rel) <= WINDOW)
    onehot = jnp.asarray(_t5_bucket(rel)[:, :, None] == np.arange(N_BUCKETS), F32)
    bias = jnp.einsum("qsb,bh->qsh", onehot, rpb_table.astype(F32),
                      precision=lax.Precision.HIGHEST)
    bias = jnp.where(band[:, :, None], bias * LOG2E, NEG_INF)
    bias = jnp.transpose(bias, (2, 0, 1))
    sink = jnp.full((B_HEADS, WIN_ROWS, LANES), NEG_INF, F32)
    sink = sink.at[:, :, 0].set(jnp.broadcast_to(b_sink_l.astype(F32)[:, None] * LOG2E, (B_HEADS, WIN_ROWS)))
    ext = jnp.concatenate([bias, sink], axis=-1)
    group = B_HEADS // B_KV_HEADS
    return ext.reshape(B_KV_HEADS, group * WIN_ROWS, span + LANES)


def _pad_cols(w, groups, width, total):
    k = w.shape[0]
    w = w.reshape(k, groups, width)
    w = jnp.pad(w, ((0, 0), (0, 0), (0, total - width)))
    return w.reshape(k, groups * total)


def _layer_params(l, w_in, a_q_norm_g, a_k_norm_g, c_q_norm_g, c_kv_norm_g, c_w_uq, c_w_ukv,
                  w_branch_a, w_branch_b, w_branch_c, w_o):
    w = w_in[l]
    splits = np.cumsum([0, A_HEADS * HEAD_DIM, A_KV_HEADS * HEAD_DIM, A_KV_HEADS * HEAD_DIM,
                        B_HEADS * HEAD_DIM, B_KV_HEADS * HEAD_DIM, B_KV_HEADS * HEAD_DIM,
                        C_Q_LORA, C_KV_LORA, C_ROPE, D_MODEL, D_MODEL, D_MODEL])
    part = [w[:, splits[i]:splits[i + 1]] for i in range(12)]
    qa, ka, va, qb, kb, vb, cq, ckv, kr, ga, gb, gc = part
    va_e = _pad_cols(va, A_KV_HEADS, HEAD_DIM, LANES)
    kr_p = jnp.pad(kr, ((0, 0), (C_NOPE, LANES - C_NOPE - C_ROPE)))
    w1 = jnp.concatenate([qa, ka, va_e, qb * (HEAD_DIM ** -0.5 * LOG2E), kb, vb, cq, ckv, kr_p],
                         axis=1).astype(BF16)
    gate_a = jnp.concatenate([jnp.tile(a_q_norm_g[l], A_HEADS) * (HEAD_DIM ** -0.5 * LOG2E),
                              jnp.tile(a_k_norm_g[l], A_KV_HEADS)])[None, :].astype(F32)
    wuq = _pad_cols(c_w_uq[l], C_HEADS, C_NOPE + C_ROPE, LANES).astype(BF16)
    ukv = c_w_ukv[l].reshape(C_KV_LORA, C_HEADS, C_NOPE + C_V)
    wuk = jnp.pad(ukv[:, :, :C_NOPE], ((0, 0), (0, 0), (0, LANES - C_NOPE))).reshape(C_KV_LORA, C_HEADS * LANES)
    wuv = jnp.pad(ukv[:, :, C_NOPE:], ((0, 0), (0, 0), (0, LANES - C_V))).reshape(C_KV_LORA, C_HEADS * LANES)
    wukv = jnp.concatenate([wuk, wuv], axis=1).astype(BF16)
    wg = jnp.concatenate([ga, gb, gc], axis=1).astype(BF16)
    return dict(w1=w1, gate_a=gate_a, gcq=c_q_norm_g[l][None, :].astype(F32),
                gckv=c_kv_norm_g[l][None, :].astype(F32), wuq=wuq, wukv=wukv, wg=wg,
                wa=w_branch_a[l].astype(BF16), wb=w_branch_b[l].astype(BF16),
                wc=w_branch_c[l].astype(BF16), wo=w_o[l].astype(BF16))


def kernel(x, ln_in_g, ln_in_b, w_in, a_q_norm_g, a_k_norm_g, b_sink, rpb_table, c_q_norm_g, c_kv_norm_g,
           c_w_uq, c_w_ukv, w_branch_a, w_branch_b, w_branch_c, w_o, ln1_g, ln1_b, router_w, router_bias,
           w_gate, w_up, w_down, ln2_g, ln2_b):
    batch, seq, d = x.shape
    depth = w_in.shape[0]
    n = batch * seq
    alpha = (2 * depth) ** 0.25
    assert d == D_MODEL and seq % PROJ_ROWS == 0 and seq % ATTN_Q_ROWS == 0
    assert seq % (WIN_ROWS * WIN_BLOCKS_PER_STEP) == 0
    assert n % DISPATCH_ROWS == 0 and n % MERGE_ROWS == 0 and n % COMBINE_ROWS == 0

    cos_a, sin_a, cos_q, sin_q, cos_k, sin_k = _rope_tables(seq)
    seg = np.arange(PA_W) // HEAD_DIM
    bd = jnp.asarray((seg[:, None] == seg[None, :]).astype(np.float32) / HEAD_DIM, BF16)
    ones_pat = np.zeros((1, LANES), np.float32)
    ones_pat[0, HEAD_DIM:] = 1.0
    vones_a = jnp.asarray(np.tile(ones_pat, (1, A_KV_HEADS)))
    vones_c = jnp.asarray(np.tile(ones_pat, (1, C_HEADS)))
    tri = jnp.asarray(np.triu(np.ones((MERGE_ROWS, MERGE_ROWS), np.float32), k=1), BF16)
    rwt = router_w.astype(F32).T
    rb = router_bias.astype(F32)[:, None]
    lng = ln_in_g.astype(F32)[None, :]
    lnb = ln_in_b.astype(F32)[None, :]

    tme = EXPERT_ROWS
    n_tiles = (TOP_K * n) // tme + N_EXPERTS
    p_rows = n_tiles * tme

    h = x.reshape(n, d).astype(F32)
    for l in range(depth):
        prm = _layer_params(l, w_in, a_q_norm_g, a_k_norm_g, c_q_norm_g, c_kv_norm_g, c_w_uq, c_w_ukv,
                            w_branch_a, w_branch_b, w_branch_c, w_o)
        outs = _proj_call(h, lng, lnb, prm["w1"], bd, prm["gate_a"], cos_a, sin_a, prm["gcq"], prm["gckv"],
                          prm["wuq"], prm["wukv"], cos_q, sin_q, cos_k, sin_k, vones_a, vones_c,
                          seq=seq, apply_ln=(l == 0))
        if l == 0:
            h, *outs = outs
        qa, ka, va, qb, kb, vb, qc, kc, vc = outs
        oa = _dense_attn_call(qa, ka, va, batch=batch, seq=seq, heads=A_HEADS, kv_heads=A_KV_HEADS,
                              dq=HEAD_DIM, name="attn_a")
        bias = _window_bias(rpb_table, b_sink[l])
        ob = _win_attn_call(qb, kb, vb, bias, batch=batch, seq=seq, heads=B_HEADS, kv_heads=B_KV_HEADS)
        oc = _dense_attn_call(qc, kc, vc, batch=batch, seq=seq, heads=C_HEADS, kv_heads=C_HEADS,
                              dq=LANES, name="attn_c")
        x1, xp, e_idx, wts, rank, cnt = _merge_call(
            h, oa, ob, oc, prm["wg"], prm["wa"], prm["wb"], prm["wc"], prm["wo"],
            ln1_g[l].astype(F32)[None, :], ln1_b[l].astype(F32)[None, :], rwt, rb, tri, alpha=alpha)

        counts = cnt[:, 0].astype(jnp.int32)
        padded = ((counts + tme - 1) // tme) * tme
        ends = jnp.cumsum(padded)
        offs = ends - padded
        pos = rank
        for e in range(N_EXPERTS):
            pos = pos + jnp.where(e_idx == e, offs[e], 0)
        tile_start = jnp.arange(n_tiles, dtype=jnp.int32) * tme
        tile_expert = jnp.minimum(jnp.sum(tile_start[:, None] >= ends[None, :], axis=1), N_EXPERTS - 1)
        tile_expert = tile_expert.astype(jnp.int32)
        n_used = (ends[-1:] // tme).astype(jnp.int32)

        def per_step(rows):
            return pos.reshape(TOP_K, n // rows, rows).transpose(1, 0, 2).reshape(n // rows, 1, TOP_K * rows)

        xs = _dispatch_call(per_step(DISPATCH_ROWS), xp, jnp.zeros((p_rows, ROW_TILES, LANES), jnp.uint32))
        ys = _expert_call(tile_expert, n_used, xs, w_gate.astype(F32), w_up.astype(F32), w_down.astype(F32),
                          layer=l)
        pos_c = per_step(COMBINE_ROWS)
        h = _combine_call(pos_c, ys, x1, wts.T, ln2_g[l].astype(F32)[None, :], ln2_b[l].astype(F32)[None, :],
                          alpha=alpha)
    return h.reshape(batch, seq, d).astype(x.dtype)
```

```python
import functools
import math

import numpy as np
import jax
import jax.numpy as jnp
from jax import lax
from jax.experimental import pallas as pl
from jax.experimental.pallas import tpu as pltpu

F32 = jnp.float32
BF16 = jnp.bfloat16

D_MODEL = 1024
GRID_W = 64
HEAD_DIM = 64
A_HEADS, A_KV_HEADS = 8, 2
B_HEADS, B_KV_HEADS = 8, 2
WINDOW = 128
C_HEADS = 8
C_NOPE, C_ROPE, C_V = 64, 32, 64
C_Q_LORA, C_KV_LORA = 384, 256
N_BUCKETS, MAX_DISTANCE = 32, 128
N_EXPERTS, N_GROUPS, TOP_K = 16, 4, 2
EXPERTS_PER_GROUP = N_EXPERTS // N_GROUPS
D_FF = 512
ROPE_THETA = 10000.0
NORM_EPS = 1e-6
NEG_INF = -1e30

LANES = 128
ROW_TILES = D_MODEL // 2 // LANES
BF16_SUBLANES = 16
VT_ROWS = HEAD_DIM + BF16_SUBLANES
LOG2E = math.log2(math.e)
VMEM_LIMIT_BYTES = 48 * 1024 * 1024

PROJ_ROWS = 512
ATTN_Q_ROWS = 512
ATTN_KEY_CHUNK = 256
ATTN_HEADS_INTERLEAVED = 1
WIN_ROWS = 128
WIN_BLOCKS_PER_STEP = 2
MERGE_ROWS = 512
MERGE_COL_TILE = 256
DISPATCH_ROWS = 1024
EXPERT_ROWS = 512
EXPERT_COL_TILE = 256
COMBINE_ROWS = 256
DMA_UNROLL = 8

PA_W = A_HEADS * HEAD_DIM + A_KV_HEADS * HEAD_DIM
PA_V = A_KV_HEADS * LANES
PB_W = B_HEADS * HEAD_DIM + 2 * B_KV_HEADS * HEAD_DIM
PC_W = C_Q_LORA + C_KV_LORA + LANES
OFF_A, OFF_AV = 0, PA_W
OFF_B = OFF_AV + PA_V
OFF_C = OFF_B + PB_W
P1_COLS = OFF_C + PC_W


def _cparams(semantics):
    return pltpu.CompilerParams(dimension_semantics=semantics, vmem_limit_bytes=VMEM_LIMIT_BYTES)


def _layer_norm(x, g, b):
    mu = jnp.mean(x, axis=-1, keepdims=True)
    xc = x - mu
    var = jnp.mean(xc * xc, axis=-1, keepdims=True)
    return xc * lax.rsqrt(var + NORM_EPS) * g + b


def _swap16(x):
    lane = lax.broadcasted_iota(jnp.int32, x.shape, x.ndim - 1)
    up = pltpu.roll(x, LANES - 16, axis=x.ndim - 1)
    down = pltpu.roll(x, 16, axis=x.ndim - 1)
    return jnp.where((lane & 16) == 0, up, down)


def _rope_tiles(x, cos, sin):
    tiles = []
    for c in range(x.shape[-1] // LANES):
        xc = x[:, c * LANES:(c + 1) * LANES]
        tiles.append(xc * cos + _swap16(xc) * sin)
    return jnp.concatenate(tiles, axis=-1) if len(tiles) > 1 else tiles[0]


def _store_vt(vt_ref, v_ext, kv_heads):
    vt = v_ext.T
    for kv in range(kv_heads):
        vt_ref[kv * VT_ROWS:(kv + 1) * VT_ROWS, :] = vt[kv * LANES:kv * LANES + VT_ROWS].astype(BF16)


def _proj_kernel(x_ref, lng_ref, lnb_ref, w1_ref, bd_ref, ga_ref, cosa_ref, sina_ref,
                 gcq_ref, gckv_ref, wuq_ref, wukv_ref, cosq_ref, sinq_ref, cosk_ref, sink_ref,
                 vones_a_ref, vones_c_ref,
                 *out_refs, apply_ln):
    if apply_ln:
        xn_ref, qa_ref, ka_ref, va_ref, qb_ref, kb_ref, vb_ref, qc_ref, kc_ref, vc_ref = out_refs
    else:
        qa_ref, ka_ref, va_ref, qb_ref, kb_ref, vb_ref, qc_ref, kc_ref, vc_ref = out_refs
    x = x_ref[...]
    if apply_ln:
        x = _layer_norm(x, lng_ref[...], lnb_ref[...])
        xn_ref[...] = x
    xb = x.astype(BF16)
    p = jnp.dot(xb, w1_ref[...], preferred_element_type=F32)

    qk = p[:, OFF_A:OFF_A + PA_W]
    ms = jnp.dot((qk * qk).astype(BF16), bd_ref[...], preferred_element_type=F32)
    qk = qk * lax.rsqrt(ms + NORM_EPS) * ga_ref[...]
    qk = _rope_tiles(qk, cosa_ref[...], sina_ref[...])
    nq = A_HEADS * HEAD_DIM
    qa_ref[...] = qk[:, :nq].astype(BF16)
    kk = qk[:, nq:]
    low = lax.broadcasted_iota(jnp.int32, kk.shape, 1) < HEAD_DIM
    ka_ref[...] = jnp.concatenate([jnp.where(low, kk, 0.0),
                                   jnp.where(low, pltpu.roll(kk, HEAD_DIM, axis=1), 0.0)],
                                  axis=-1).astype(BF16)
    _store_vt(va_ref, p[:, OFF_AV:OFF_AV + PA_V] + vones_a_ref[...], A_KV_HEADS)

    nqb = B_HEADS * HEAD_DIM
    nkb = B_KV_HEADS * HEAD_DIM
    qb_ref[...] = p[:, OFF_B:OFF_B + nqb].astype(BF16)
    kb_ref[...] = p[:, OFF_B + nqb:OFF_B + nqb + nkb].astype(BF16)
    vb_ref[...] = p[:, OFF_B + nqb + nkb:OFF_B + PB_W].astype(BF16)

    cq = p[:, OFF_C:OFF_C + C_Q_LORA]
    ckv = p[:, OFF_C + C_Q_LORA:OFF_C + C_Q_LORA + C_KV_LORA]
    kr = p[:, OFF_C + C_Q_LORA + C_KV_LORA:OFF_C + PC_W]
    cqn = cq * lax.rsqrt(jnp.mean(cq * cq, axis=-1, keepdims=True) + NORM_EPS) * gcq_ref[...]
    ckvn = ckv * lax.rsqrt(jnp.mean(ckv * ckv, axis=-1, keepdims=True) + NORM_EPS) * gckv_ref[...]
    qc = jnp.dot(cqn.astype(BF16), wuq_ref[...], preferred_element_type=F32)
    qc_ref[...] = _rope_tiles(qc, cosq_ref[...], sinq_ref[...]).astype(BF16)
    kv = jnp.dot(ckvn.astype(BF16), wukv_ref[...], preferred_element_type=F32)
    kpe = _rope_tiles(kr, cosk_ref[...], sink_ref[...])
    hw = C_HEADS * LANES
    kc_ref[...] = (kv[:, :hw] + jnp.concatenate([kpe] * C_HEADS, axis=-1)).astype(BF16)
    _store_vt(vc_ref, kv[:, hw:] + vones_c_ref[...], C_HEADS)


def _proj_call(x, lng, lnb, w1, bd, ga, cosa, sina, gcq, gckv, wuq, wukv, cosq, sinq, cosk, sink,
               vones_a, vones_c, *, seq, apply_ln):
    n = x.shape[0]
    tm = PROJ_ROWS
    nsb = seq // tm
    row = lambda w: pl.BlockSpec((tm, w), lambda i: (i, 0))
    full = lambda a: pl.BlockSpec(a.shape, lambda i: (0,) * a.ndim)
    tab = pl.BlockSpec((tm, LANES), lambda i: (i % nsb, 0))
    out_w = [A_HEADS * HEAD_DIM, A_KV_HEADS * LANES, -A_KV_HEADS * VT_ROWS,
             B_HEADS * HEAD_DIM, B_KV_HEADS * HEAD_DIM, B_KV_HEADS * HEAD_DIM,
             C_HEADS * LANES, C_HEADS * LANES, -C_HEADS * VT_ROWS]
    out_shape = [jax.ShapeDtypeStruct((n, w) if w > 0 else (-w, n), BF16) for w in out_w]
    out_specs = [row(w) if w > 0 else pl.BlockSpec((-w, tm), lambda i: (0, i)) for w in out_w]
    if apply_ln:
        out_shape = [jax.ShapeDtypeStruct((n, D_MODEL), F32)] + out_shape
        out_specs = [row(D_MODEL)] + out_specs
    return pl.pallas_call(
        functools.partial(_proj_kernel, apply_ln=apply_ln),
        out_shape=out_shape,
        grid=(n // tm,),
        in_specs=[row(D_MODEL), full(lng), full(lnb), full(w1), full(bd), full(ga), tab, tab,
                  full(gcq), full(gckv), full(wuq), full(wukv), tab, tab, tab, tab,
                  full(vones_a), full(vones_c)],
        out_specs=out_specs,
        compiler_params=_cparams(("parallel",)),
        name="proj_ln" if apply_ln else "proj",
    )(x, lng, lnb, w1, bd, ga, cosa, sina, gcq, gckv, wuq, wukv, cosq, sinq, cosk, sink,
      vones_a, vones_c)


def _dense_attn_kernel(q_ref, k_ref, vt_ref, o_ref, *, heads, kv_heads, dq):
    group = heads // kv_heads
    seq = k_ref.shape[0]
    tq = q_ref.shape[0]
    n_chunks = seq // ATTN_KEY_CHUNK
    units = [(h0 + dh, c) for h0 in range(0, heads, ATTN_HEADS_INTERLEAVED) for c in range(n_chunks)
             for dh in range(ATTN_HEADS_INTERLEAVED)]

    def scores_t(h, c):
        kv = h // group
        qh = q_ref[:, h * dq:(h + 1) * dq]
        kc = k_ref[c * ATTN_KEY_CHUNK:(c + 1) * ATTN_KEY_CHUNK, kv * LANES:(kv + 1) * LANES]
        if dq < LANES:
            kc = kc[:, :dq]
        return lax.dot_general(kc, qh, (((1,), (1,)), ((), ())), preferred_element_type=F32)

    ahead = 2
    st = {u: scores_t(*u) for u in units[:ahead]}
    outs = [None] * heads
    m, acc = {}, {}
    for idx, (h, c) in enumerate(units):
        kv = h // group
        s = st.pop((h, c))
        if idx + ahead < len(units):
            st[units[idx + ahead]] = scores_t(*units[idx + ahead])
        cmax = jnp.max(s, axis=0, keepdims=True)
        m_new = cmax if c == 0 else jnp.maximum(m[h], cmax)
        pt = jnp.exp2(s - m_new).astype(BF16)
        vtc = vt_ref[kv * VT_ROWS:(kv + 1) * VT_ROWS, c * ATTN_KEY_CHUNK:(c + 1) * ATTN_KEY_CHUNK]
        pv = jnp.dot(vtc, pt, preferred_element_type=F32)
        acc[h] = pv if c == 0 else acc[h] * jnp.exp2(m[h] - m_new) + pv
        m[h] = m_new
        if c == n_chunks - 1:
            outs[h] = acc[h][:HEAD_DIM] / acc[h][HEAD_DIM:HEAD_DIM + 1]
    o_ref[...] = jnp.concatenate(outs, axis=0).T.astype(o_ref.dtype)


def _dense_attn_call(q, k, vt, *, batch, seq, heads, kv_heads, dq, name):
    n = q.shape[0]
    tq = ATTN_Q_ROWS
    nq = seq // tq
    return pl.pallas_call(
        functools.partial(_dense_attn_kernel, heads=heads, kv_heads=kv_heads, dq=dq),
        out_shape=jax.ShapeDtypeStruct((n, heads * HEAD_DIM), BF16),
        grid=(batch, nq),
        in_specs=[pl.BlockSpec((tq, heads * dq), lambda b, i: (b * nq + i, 0)),
                  pl.BlockSpec((seq, kv_heads * LANES), lambda b, i: (b, 0)),
                  pl.BlockSpec((kv_heads * VT_ROWS, seq), lambda b, i: (0, b))],
        out_specs=pl.BlockSpec((tq, heads * HEAD_DIM), lambda b, i: (b * nq + i, 0)),
        compiler_params=_cparams(("parallel", "parallel")),
        name=name,
    )(q, k, vt)


def _win_attn_kernel(q_ref, k_ref, v_ref, bias_ref, o_ref, *, heads, kv_heads, nb):
    group = heads // kv_heads
    tq = WIN_ROWS
    for j in range(WIN_BLOCKS_PER_STEP):
        i = pl.program_id(1) * WIN_BLOCKS_PER_STEP + j
        outs = [None] * heads
        for kv in range(kv_heads):
            qs = jnp.concatenate(
                [q_ref[j * tq:(j + 1) * tq, (kv * group + g) * HEAD_DIM:(kv * group + g + 1) * HEAD_DIM]
                 for g in range(group)], axis=0)
            bias = bias_ref[kv]
            logit_chunks, v_chunks = [], []
            for c in range(3):
                blk = i + (c - 1)
                start = pl.multiple_of(jnp.clip(blk, 0, nb - 1) * tq, tq)
                kc = k_ref[pl.ds(start, tq), kv * HEAD_DIM:(kv + 1) * HEAD_DIM]
                v_chunks.append(v_ref[pl.ds(start, tq), kv * HEAD_DIM:(kv + 1) * HEAD_DIM])
                sc = lax.dot_general(qs, kc, (((1,), (1,)), ((), ())), preferred_element_type=F32)
                sc = sc + bias[:, c * tq:(c + 1) * tq]
                in_range = jnp.logical_and(blk >= 0, blk <= nb - 1)
                logit_chunks.append(jnp.where(in_range, sc, NEG_INF))
            logit_chunks.append(bias[:, 3 * tq:])
            logits = jnp.concatenate(logit_chunks, axis=-1)
            m = jnp.max(logits, axis=-1, keepdims=True)
            pr = jnp.exp2(logits - m)
            denom = jnp.sum(pr, axis=-1, keepdims=True)
            vv = jnp.concatenate(v_chunks, axis=0)
            o = jnp.dot(pr[:, :3 * tq].astype(BF16), vv, preferred_element_type=F32) / denom
            for g in range(group):
                outs[kv * group + g] = o[g * tq:(g + 1) * tq]
        o_ref[j * tq:(j + 1) * tq, :] = jnp.concatenate(outs, axis=-1).astype(o_ref.dtype)


def _win_attn_call(q, k, v, bias, *, batch, seq, heads, kv_heads):
    n = q.shape[0]
    ts = WIN_ROWS * WIN_BLOCKS_PER_STEP
    nq = seq // ts
    return pl.pallas_call(
        functools.partial(_win_attn_kernel, heads=heads, kv_heads=kv_heads, nb=seq // WIN_ROWS),
        out_shape=jax.ShapeDtypeStruct((n, heads * HEAD_DIM), BF16),
        grid=(batch, nq),
        in_specs=[pl.BlockSpec((ts, heads * HEAD_DIM), lambda b, i: (b * nq + i, 0)),
                  pl.BlockSpec((seq, kv_heads * HEAD_DIM), lambda b, i: (b, 0)),
                  pl.BlockSpec((seq, kv_heads * HEAD_DIM), lambda b, i: (b, 0)),
                  pl.BlockSpec(bias.shape, lambda b, i: (0, 0, 0))],
        out_specs=pl.BlockSpec((ts, heads * HEAD_DIM), lambda b, i: (b * nq + i, 0)),
        compiler_params=_cparams(("parallel", "parallel")),
        name="win_attn",
    )(q, k, v, bias)


def _pack_rows(x):
    half = x.shape[-1] // 2
    hi = pltpu.bitcast(x[:, :half].astype(BF16).astype(F32), jnp.uint32)
    lo = pltpu.bitcast(x[:, half:].astype(BF16).astype(F32), jnp.uint32)
    return hi | (lo >> 16)


def _unpack_rows(u):
    hi = pltpu.bitcast(u & jnp.uint32(0xFFFF0000), F32)
    lo = pltpu.bitcast(u << 16, F32)
    return jnp.concatenate([hi, lo], axis=-1)


def _store_row_tiles(ref, u):
    for c in range(ROW_TILES):
        ref[:, c, :] = u[:, c * LANES:(c + 1) * LANES]


def _load_row_tiles(ref):
    return jnp.concatenate([ref[:, c, :] for c in range(ROW_TILES)], axis=-1)


def _merge_kernel(x_ref, oa_ref, ob_ref, oc_ref, wg_ref, wa_ref, wb_ref, wc_ref, wo_ref,
                  g_ref, b_ref, rwt_ref, rb_ref, tri_ref,
                  x1_ref, xp_ref, e_ref, w_ref, r_ref, cnt_ref, carry_ref, *, alpha):
    step = pl.program_id(0)

    @pl.when(step == 0)
    def _():
        carry_ref[...] = jnp.zeros_like(carry_ref)

    x = x_ref[...]
    xb = x.astype(BF16)
    d = D_MODEL
    mixers = ((oa_ref[...], wa_ref), (ob_ref[...], wb_ref), (oc_ref[...], wc_ref))
    tiles = []
    for c0 in range(0, d, MERGE_COL_TILE):
        acc = None
        for idx, (o, wbr_ref) in enumerate(mixers):
            gate = jax.nn.sigmoid(jnp.dot(xb, wg_ref[:, idx * d + c0:idx * d + c0 + MERGE_COL_TILE],
                                          preferred_element_type=F32))
            br = jnp.dot(o, wbr_ref[:, c0:c0 + MERGE_COL_TILE], preferred_element_type=F32)
            acc = gate * br if acc is None else acc + gate * br
        tiles.append(acc.astype(BF16))
    merged = jnp.concatenate(tiles, axis=-1)
    mix = jnp.dot(merged, wo_ref[...], preferred_element_type=F32)
    x1 = _layer_norm(alpha * x + mix, g_ref[...], b_ref[...])
    x1_ref[...] = x1
    _store_row_tiles(xp_ref, _pack_rows(x1))

    logits = lax.dot_general(rwt_ref[...], x1, (((1,), (1,)), ((), ())),
                             preferred_element_type=F32, precision=lax.Precision.HIGHEST)
    scores = jax.nn.sigmoid(logits)
    biased = scores + rb_ref[...]
    shape = biased.shape
    rows = lax.broadcasted_iota(jnp.int32, shape, 0)
    sub = lax.broadcasted_iota(jnp.int32, (EXPERTS_PER_GROUP, shape[1]), 0)
    best, sel = None, None
    for g in range(N_GROUPS):
        blk = biased[g * EXPERTS_PER_GROUP:(g + 1) * EXPERTS_PER_GROUP]
        m1 = jnp.max(blk, axis=0, keepdims=True)
        i1 = jnp.min(jnp.where(blk == m1, sub, EXPERTS_PER_GROUP), axis=0, keepdims=True)
        m2 = jnp.max(jnp.where(sub == i1, -jnp.inf, blk), axis=0, keepdims=True)
        gs = m1 + m2
        if best is None:
            best, sel = gs, jnp.zeros_like(i1)
        else:
            better = gs > best
            sel = jnp.where(better, g, sel)
            best = jnp.where(better, gs, best)
    in_group = (rows >> 2) == sel
    bm = jnp.where(in_group, biased, -jnp.inf)
    t1 = jnp.max(bm, axis=0, keepdims=True)
    e0 = jnp.min(jnp.where(bm == t1, rows, N_EXPERTS), axis=0, keepdims=True)
    bm2 = jnp.where(rows == e0, -jnp.inf, bm)
    t2 = jnp.max(bm2, axis=0, keepdims=True)
    e1 = jnp.min(jnp.where(bm2 == t2, rows, N_EXPERTS), axis=0, keepdims=True)
    hot0 = rows == e0
    hot1 = rows == e1
    s0 = jnp.sum(jnp.where(hot0, scores, 0.0), axis=0, keepdims=True)
    s1 = jnp.sum(jnp.where(hot1, scores, 0.0), axis=0, keepdims=True)
    tot = s0 + s1
    e_ref[...] = jnp.concatenate([e0, e1], axis=0)
    w_ref[...] = jnp.concatenate([s0 / tot, s1 / tot], axis=0)

    hot = jnp.where(jnp.logical_or(hot0, hot1), 1.0, 0.0)
    prefix = jnp.dot(hot.astype(BF16), tri_ref[...], preferred_element_type=F32) + carry_ref[:, 0:1]
    r0 = jnp.sum(jnp.where(hot0, prefix, 0.0), axis=0, keepdims=True)
    r1 = jnp.sum(jnp.where(hot1, prefix, 0.0), axis=0, keepdims=True)
    r_ref[...] = jnp.concatenate([r0, r1], axis=0).astype(jnp.int32)
    carry_ref[...] = carry_ref[...] + jnp.sum(hot, axis=1, keepdims=True)
    cnt_ref[...] = carry_ref[...]


def _merge_call(x, oa, ob, oc, wg, wa, wb, wc, wo, g, b, rwt, rb, tri, *, alpha):
    n = x.shape[0]
    tm = MERGE_ROWS
    row = lambda w: pl.BlockSpec((tm, w), lambda i: (i, 0))
    full = lambda a: pl.BlockSpec(a.shape, lambda i: (0,) * a.ndim)
    tok = pl.BlockSpec((TOP_K, tm), lambda i: (0, i))
    packed = pl.BlockSpec((tm, ROW_TILES, LANES), lambda i: (i, 0, 0))
    return pl.pallas_call(
        functools.partial(_merge_kernel, alpha=alpha),
        out_shape=[jax.ShapeDtypeStruct((n, D_MODEL), F32),
                   jax.ShapeDtypeStruct((n, ROW_TILES, LANES), jnp.uint32),
                   jax.ShapeDtypeStruct((TOP_K, n), jnp.int32),
                   jax.ShapeDtypeStruct((TOP_K, n), F32),
                   jax.ShapeDtypeStruct((TOP_K, n), jnp.int32),
                   jax.ShapeDtypeStruct((N_EXPERTS, LANES), F32)],
        grid=(n // tm,),
        in_specs=[row(D_MODEL), row(oa.shape[1]), row(ob.shape[1]), row(oc.shape[1]),
                  full(wg), full(wa), full(wb), full(wc), full(wo), full(g), full(b),
                  full(rwt), full(rb), full(tri)],
        out_specs=[row(D_MODEL), packed, tok, tok, tok,
                   pl.BlockSpec((N_EXPERTS, LANES), lambda i: (0, 0))],
        scratch_shapes=[pltpu.VMEM((N_EXPERTS, LANES), F32)],
        compiler_params=_cparams(("arbitrary",)),
        name="merge",
    )(x, oa, ob, oc, wg, wa, wb, wc, wo, g, b, rwt, rb, tri)


def _dispatch_kernel(pos_ref, x_ref, dst_in_hbm, dst_hbm, sem):
    del dst_in_hbm
    tb = DISPATCH_ROWS

    def start(g, carry):
        for u in range(DMA_UNROLL):
            t = g * DMA_UNROLL + u
            for j in range(TOP_K):
                pltpu.make_async_copy(x_ref.at[t], dst_hbm.at[pos_ref[0, 0, j * tb + t]],
                                      sem).start(priority=(u + j) % 2)
        return carry

    lax.fori_loop(0, tb // DMA_UNROLL, start, 0)
    for j in range(TOP_K):
        pltpu.make_async_copy(x_ref, dst_hbm.at[pl.ds(0, tb)], sem).wait()


def _dispatch_call(pos_flat, xp, dst0):
    nsteps = pos_flat.shape[0]
    tb = DISPATCH_ROWS
    return pl.pallas_call(
        _dispatch_kernel,
        out_shape=jax.ShapeDtypeStruct(dst0.shape, dst0.dtype),
        grid=(nsteps,),
        in_specs=[pl.BlockSpec((1, 1, TOP_K * tb), lambda i: (i, 0, 0), memory_space=pltpu.SMEM),
                  pl.BlockSpec((tb, ROW_TILES, LANES), lambda i: (i, 0, 0)),
                  pl.BlockSpec(memory_space=pl.ANY)],
        out_specs=pl.BlockSpec(memory_space=pl.ANY),
        scratch_shapes=[pltpu.SemaphoreType.DMA(())],
        input_output_aliases={2: 0},
        compiler_params=pltpu.CompilerParams(dimension_semantics=("arbitrary",), has_side_effects=True,
                                             vmem_limit_bytes=VMEM_LIMIT_BYTES),
        name="dispatch",
    )(pos_flat, xp, dst0)


def _expert_kernel(te_ref, nu_ref, xs_ref, wg_ref, wu_ref, wd_ref, ys_ref, wgu_bf, wd_bf):
    i = pl.program_id(0)

    @pl.when(jnp.logical_or(i == 0, te_ref[i] != te_ref[jnp.maximum(i - 1, 0)]))
    def _():
        wgu_bf[:, :D_FF] = wg_ref[0, 0].astype(BF16)
        wgu_bf[:, D_FF:] = wu_ref[0, 0].astype(BF16)
        wd_bf[...] = wd_ref[0, 0].astype(BF16)

    @pl.when(i < nu_ref[0])
    def _():
        xb = _unpack_rows(_load_row_tiles(xs_ref)).astype(BF16)
        hid = []
        for c0 in range(0, D_FF, EXPERT_COL_TILE):
            gate = jnp.dot(xb, wgu_bf[:, c0:c0 + EXPERT_COL_TILE], preferred_element_type=F32)
            up = jnp.dot(xb, wgu_bf[:, D_FF + c0:D_FF + c0 + EXPERT_COL_TILE], preferred_element_type=F32)
            hid.append(((gate * jax.nn.sigmoid(gate)) * up).astype(BF16))
        y = jnp.dot(jnp.concatenate(hid, axis=-1), wd_bf[...], preferred_element_type=F32)
        _store_row_tiles(ys_ref, _pack_rows(y))

    @pl.when(i >= nu_ref[0])
    def _():
        ys_ref[...] = jnp.zeros_like(ys_ref)


def _expert_call(tile_expert, n_used, xs, w_gate, w_up, w_down, *, layer):
    p = xs.shape[0]
    tm = EXPERT_ROWS
    wspec = lambda r, c: pl.BlockSpec((1, 1, r, c), lambda i, te, nu: (layer, te[i], 0, 0))
    grid_spec = pltpu.PrefetchScalarGridSpec(
        num_scalar_prefetch=2,
        grid=(p // tm,),
        in_specs=[pl.BlockSpec((tm, ROW_TILES, LANES), lambda i, te, nu: (i, 0, 0)),
                  wspec(D_MODEL, D_FF), wspec(D_MODEL, D_FF), wspec(D_FF, D_MODEL)],
        out_specs=pl.BlockSpec((tm, ROW_TILES, LANES), lambda i, te, nu: (i, 0, 0)),
        scratch_shapes=[pltpu.VMEM((D_MODEL, 2 * D_FF), BF16), pltpu.VMEM((D_FF, D_MODEL), BF16)],
    )
    return pl.pallas_call(
        _expert_kernel,
        out_shape=jax.ShapeDtypeStruct((p, ROW_TILES, LANES), jnp.uint32),
        grid_spec=grid_spec,
        compiler_params=_cparams(("arbitrary",)),
        name="experts",
    )(tile_expert, n_used, xs, w_gate, w_up, w_down)


def _combine_kernel(pos_ref, nxt_ref, ys_hbm, x_ref, w_ref, g_ref, b_ref, o_ref, buf_ref, sem, *, alpha):
    tc = COMBINE_ROWS
    i = pl.program_id(0)
    slot = i % 2

    def row_copy(idx_ref, to_slot, t, j):
        return pltpu.make_async_copy(ys_hbm.at[idx_ref[0, 0, j * tc + t]], buf_ref.at[to_slot, j, t],
                                     sem.at[to_slot])

    def wait_slot(s):
        for j in range(TOP_K):
            pltpu.make_async_copy(ys_hbm.at[pl.ds(0, tc)], buf_ref.at[s, j], sem.at[s]).wait()

    @pl.when(i == 0)
    def _():
        def start(g, carry):
            for u in range(DMA_UNROLL):
                for j in range(TOP_K):
                    row_copy(pos_ref, 0, g * DMA_UNROLL + u, j).start(priority=(u + j) % 2)
            return carry
        lax.fori_loop(0, tc // DMA_UNROLL, start, 0)

    for t in range(tc):
        for j in range(TOP_K):
            row_copy(nxt_ref, 1 - slot, t, j).start(priority=(t + j) % 2)
    wait_slot(slot)
    w = w_ref[...]
    ffn = (w[:, 0:1] * _unpack_rows(_load_row_tiles(buf_ref.at[slot, 0]))
           + w[:, 1:2] * _unpack_rows(_load_row_tiles(buf_ref.at[slot, 1])))
    o_ref[...] = _layer_norm(alpha * x_ref[...] + ffn, g_ref[...], b_ref[...])

    @pl.when(i == pl.num_programs(0) - 1)
    def _():
        wait_slot(1 - slot)


def _combine_call(pos_flat, ys, x1, wts_t, g, b, *, alpha):
    n = x1.shape[0]
    tc = COMBINE_ROWS
    nsteps = n // tc
    idx = lambda shift: pl.BlockSpec((1, 1, TOP_K * tc), lambda i: (jnp.minimum(i + shift, nsteps - 1), 0, 0),
                                     memory_space=pltpu.SMEM)
    return pl.pallas_call(
        functools.partial(_combine_kernel, alpha=alpha),
        out_shape=jax.ShapeDtypeStruct((n, D_MODEL), F32),
        grid=(nsteps,),
        in_specs=[idx(0), idx(1),
                  pl.BlockSpec(memory_space=pl.ANY),
                  pl.BlockSpec((tc, D_MODEL), lambda i: (i, 0)),
                  pl.BlockSpec((tc, TOP_K), lambda i: (i, 0)),
                  pl.BlockSpec(g.shape, lambda i: (0, 0)),
                  pl.BlockSpec(b.shape, lambda i: (0, 0))],
        out_specs=pl.BlockSpec((tc, D_MODEL), lambda i: (i, 0)),
        scratch_shapes=[pltpu.VMEM((2, TOP_K, tc, ROW_TILES, LANES), jnp.uint32),
                        pltpu.SemaphoreType.DMA((2,))],
        compiler_params=_cparams(("arbitrary",)),
        name="combine",
    )(pos_flat, pos_flat, ys, x1, wts_t, g, b)


def _t5_bucket(rel):
    nb = N_BUCKETS // 2
    ret = (rel > 0).astype(np.int32) * nb
    n = np.abs(rel)
    max_exact = nb // 2
    large = max_exact + (np.log(np.maximum(n, 1) / max_exact)
                         / math.log(MAX_DISTANCE / max_exact) * (nb - max_exact)).astype(np.int32)
    large = np.minimum(large, nb - 1)
    return (ret + np.where(n < max_exact, n, large)).astype(np.int32)


def _rope_tables(seq):
    def cs(pos, dim):
        inv = ROPE_THETA ** (-(jnp.arange(0, dim, 2, dtype=F32) / dim))
        ang = pos.astype(F32)[:, None] * inv[None, :]
        return jnp.cos(ang), jnp.sin(ang)

    t = jnp.arange(seq)
    cr, sr = cs(t // GRID_W, HEAD_DIM // 2)
    cc, sc = cs(t % GRID_W, HEAD_DIM // 2)
    cos_head = jnp.concatenate([cr, cr, cc, cc], axis=-1)
    sin_head = jnp.concatenate([-sr, sr, -sc, sc], axis=-1)
    cos_a = jnp.concatenate([cos_head, cos_head], axis=-1)
    sin_a = jnp.concatenate([sin_head, sin_head], axis=-1)
    cs_, ss_ = cs(t, C_ROPE)
    ones = jnp.ones((seq, C_NOPE), F32)
    pad1 = jnp.ones((seq, LANES - C_NOPE - C_ROPE), F32)
    cos_k = jnp.concatenate([ones, cs_, cs_, pad1], axis=-1)
    sin_k = jnp.concatenate([0 * ones, -ss_, ss_, 0 * pad1], axis=-1)
    scale = (C_NOPE + C_ROPE) ** -0.5 * LOG2E
    return cos_a, sin_a, cos_k * scale, sin_k * scale, cos_k, sin_k


def _window_bias(rpb_table, b_sink_l):
    span = WIN_ROWS + 2 * WINDOW
    rel = np.arange(span)[None, :] - WINDOW - np.arange(WIN_ROWS)[:, None]
    band = jnp.asarray(np.abs(rel) <= WINDOW)
    onehot = jnp.asarray(_t5_bucket(rel)[:, :, None] == np.arange(N_BUCKETS), F32)
    bias = jnp.einsum("qsb,bh->qsh", onehot, rpb_table.astype(F32),
                      precision=lax.Precision.HIGHEST)
    bias = jnp.where(band[:, :, None], bias * LOG2E, NEG_INF)
    bias = jnp.transpose(bias, (2, 0, 1))
    sink = jnp.full((B_HEADS, WIN_ROWS, LANES), NEG_INF, F32)
    sink = sink.at[:, :, 0].set(jnp.broadcast_to(b_sink_l.astype(F32)[:, None] * LOG2E, (B_HEADS, WIN_ROWS)))
    ext = jnp.concatenate([bias, sink], axis=-1)
    group = B_HEADS // B_KV_HEADS
    return ext.reshape(B_KV_HEADS, group * WIN_ROWS, span + LANES)


def _pad_cols(w, groups, width, total):
    k = w.shape[0]
    w = w.reshape(k, groups, width)
    w = jnp.pad(w, ((0, 0), (0, 0), (0, total - width)))
    return w.reshape(k, groups * total)


def _layer_params(l, w_in, a_q_norm_g, a_k_norm_g, c_q_norm_g, c_kv_norm_g, c_w_uq, c_w_ukv,
                  w_branch_a, w_branch_b, w_branch_c, w_o):
    w = w_in[l]
    splits = np.cumsum([0, A_HEADS * HEAD_DIM, A_KV_HEADS * HEAD_DIM, A_KV_HEADS * HEAD_DIM,
                        B_HEADS * HEAD_DIM, B_KV_HEADS * HEAD_DIM, B_KV_HEADS * HEAD_DIM,
                        C_Q_LORA, C_KV_LORA, C_ROPE, D_MODEL, D_MODEL, D_MODEL])
    part = [w[:, splits[i]:splits[i + 1]] for i in range(12)]
    qa, ka, va, qb, kb, vb, cq, ckv, kr, ga, gb, gc = part
    va_e = _pad_cols(va, A_KV_HEADS, HEAD_DIM, LANES)
    kr_p = jnp.pad(kr, ((0, 0), (C_NOPE, LANES - C_NOPE - C_ROPE)))
    w1 = jnp.concatenate([qa, ka, va_e, qb * (HEAD_DIM ** -0.5 * LOG2E), kb, vb, cq, ckv, kr_p],
                         axis=1).astype(BF16)
    gate_a = jnp.concatenate([jnp.tile(a_q_norm_g[l], A_HEADS) * (HEAD_DIM ** -0.5 * LOG2E),
                              jnp.tile(a_k_norm_g[l], A_KV_HEADS)])[None, :].astype(F32)
    wuq = _pad_cols(c_w_uq[l], C_HEADS, C_NOPE + C_ROPE, LANES).astype(BF16)
    ukv = c_w_ukv[l].reshape(C_KV_LORA, C_HEADS, C_NOPE + C_V)
    wuk = jnp.pad(ukv[:, :, :C_NOPE], ((0, 0), (0, 0), (0, LANES - C_NOPE))).reshape(C_KV_LORA, C_HEADS * LANES)
    wuv = jnp.pad(ukv[:, :, C_NOPE:], ((0, 0), (0, 0), (0, LANES - C_V))).reshape(C_KV_LORA, C_HEADS * LANES)
    wukv = jnp.concatenate([wuk, wuv], axis=1).astype(BF16)
    wg = jnp.concatenate([ga, gb, gc], axis=1).astype(BF16)
    return dict(w1=w1, gate_a=gate_a, gcq=c_q_norm_g[l][None, :].astype(F32),
                gckv=c_kv_norm_g[l][None, :].astype(F32), wuq=wuq, wukv=wukv, wg=wg,
                wa=w_branch_a[l].astype(BF16), wb=w_branch_b[l].astype(BF16),
                wc=w_branch_c[l].astype(BF16), wo=w_o[l].astype(BF16))


def kernel(x, ln_in_g, ln_in_b, w_in, a_q_norm_g, a_k_norm_g, b_sink, rpb_table, c_q_norm_g, c_kv_norm_g,
           c_w_uq, c_w_ukv, w_branch_a, w_branch_b, w_branch_c, w_o, ln1_g, ln1_b, router_w, router_bias,
           w_gate, w_up, w_down, ln2_g, ln2_b):
    batch, seq, d = x.shape
    depth = w_in.shape[0]
    n = batch * seq
    alpha = (2 * depth) ** 0.25
    assert d == D_MODEL and seq % PROJ_ROWS == 0 and seq % ATTN_Q_ROWS == 0
    assert seq % (WIN_ROWS * WIN_BLOCKS_PER_STEP) == 0
    assert n % DISPATCH_ROWS == 0 and n % MERGE_ROWS == 0 and n % COMBINE_ROWS == 0

    cos_a, sin_a, cos_q, sin_q, cos_k, sin_k = _rope_tables(seq)
    seg = np.arange(PA_W) // HEAD_DIM
    bd = jnp.asarray((seg[:, None] == seg[None, :]).astype(np.float32) / HEAD_DIM, BF16)
    ones_pat = np.zeros((1, LANES), np.float32)
    ones_pat[0, HEAD_DIM:] = 1.0
    vones_a = jnp.asarray(np.tile(ones_pat, (1, A_KV_HEADS)))
    vones_c = jnp.asarray(np.tile(ones_pat, (1, C_HEADS)))
    tri = jnp.asarray(np.triu(np.ones((MERGE_ROWS, MERGE_ROWS), np.float32), k=1), BF16)
    rwt = router_w.astype(F32).T
    rb = router_bias.astype(F32)[:, None]
    lng = ln_in_g.astype(F32)[None, :]
    lnb = ln_in_b.astype(F32)[None, :]

    tme = EXPERT_ROWS
    n_tiles = (TOP_K * n) // tme + N_EXPERTS
    p_rows = n_tiles * tme

    h = x.reshape(n, d).astype(F32)
    for l in range(depth):
        prm = _layer_params(l, w_in, a_q_norm_g, a_k_norm_g, c_q_norm_g, c_kv_norm_g, c_w_uq, c_w_ukv,
                            w_branch_a, w_branch_b, w_branch_c, w_o)
        outs = _proj_call(h, lng, lnb, prm["w1"], bd, prm["gate_a"], cos_a, sin_a, prm["gcq"], prm["gckv"],
                          prm["wuq"], prm["wukv"], cos_q, sin_q, cos_k, sin_k, vones_a, vones_c,
                          seq=seq, apply_ln=(l == 0))
        if l == 0:
            h, *outs = outs
        qa, ka, va, qb, kb, vb, qc, kc, vc = outs
        oa = _dense_attn_call(qa, ka, va, batch=batch, seq=seq, heads=A_HEADS, kv_heads=A_KV_HEADS,
                              dq=HEAD_DIM, name="attn_a")
        bias = _window_bias(rpb_table, b_sink[l])
        ob = _win_attn_call(qb, kb, vb, bias, batch=batch, seq=seq, heads=B_HEADS, kv_heads=B_KV_HEADS)
        oc = _dense_attn_call(qc, kc, vc, batch=batch, seq=seq, heads=C_HEADS, kv_heads=C_HEADS,
                              dq=LANES, name="attn_c")
        x1, xp, e_idx, wts, rank, cnt = _merge_call(
            h, oa, ob, oc, prm["wg"], prm["wa"], prm["wb"], prm["wc"], prm["wo"],
            ln1_g[l].astype(F32)[None, :], ln1_b[l].astype(F32)[None, :], rwt, rb, tri, alpha=alpha)

        counts = cnt[:, 0].astype(jnp.int32)
        padded = ((counts + tme - 1) // tme) * tme
        ends = jnp.cumsum(padded)
        offs = ends - padded
        pos = rank
        for e in range(N_EXPERTS):
            pos = pos + jnp.where(e_idx == e, offs[e], 0)
        tile_start = jnp.arange(n_tiles, dtype=jnp.int32) * tme
        tile_expert = jnp.minimum(jnp.sum(tile_start[:, None] >= ends[None, :], axis=1), N_EXPERTS - 1)
        tile_expert = tile_expert.astype(jnp.int32)
        n_used = (ends[-1:] // tme).astype(jnp.int32)

        def per_step(rows):
            return pos.reshape(TOP_K, n // rows, rows).transpose(1, 0, 2).reshape(n // rows, 1, TOP_K * rows)

        xs = _dispatch_call(per_step(DISPATCH_ROWS), xp, jnp.zeros((p_rows, ROW_TILES, LANES), jnp.uint32))
        ys = _expert_call(tile_expert, n_used, xs, w_gate.astype(F32), w_up.astype(F32), w_down.astype(F32),
                          layer=l)
        pos_c = per_step(COMBINE_ROWS)
        h = _combine_call(pos_c, ys, x1, wts.T, ln2_g[l].astype(F32)[None, :], ln2_b[l].astype(F32)[None, :],
                          alpha=alpha)
    return h.reshape(batch, seq, d).astype(x.dtype)
```

```python
import functools
import math

import numpy as np
import jax
import jax.numpy as jnp
from jax import lax
from jax.experimental import pallas as pl
from jax.experimental.pallas import tpu as pltpu

F32 = jnp.float32
BF16 = jnp.bfloat16

D_MODEL = 1024
GRID_W = 64
HEAD_DIM = 64
A_HEADS, A_KV_HEADS = 8, 2
B_HEADS, B_KV_HEADS = 8, 2
WINDOW = 128
C_HEADS = 8
C_NOPE, C_ROPE, C_V = 64, 32, 64
C_Q_LORA, C_KV_LORA = 384, 256
N_BUCKETS, MAX_DISTANCE = 32, 128
N_EXPERTS, N_GROUPS, TOP_K = 16, 4, 2
EXPERTS_PER_GROUP = N_EXPERTS // N_GROUPS
D_FF = 512
ROPE_THETA = 10000.0
NORM_EPS = 1e-6
NEG_INF = -1e30

LANES = 128
ROW_TILES = D_MODEL // 2 // LANES
BF16_SUBLANES = 16
VT_ROWS = HEAD_DIM + BF16_SUBLANES
LOG2E = math.log2(math.e)
VMEM_LIMIT_BYTES = 48 * 1024 * 1024

PROJ_ROWS = 512
ATTN_Q_ROWS = 512
ATTN_KEY_CHUNK = 256
ATTN_HEADS_INTERLEAVED = 1
WIN_ROWS = 128
WIN_BLOCKS_PER_STEP = 2
MERGE_ROWS = 512
MERGE_COL_TILE = 256
DISPATCH_ROWS = 1024
EXPERT_ROWS = 512
EXPERT_COL_TILE = 256
COMBINE_ROWS = 256
DMA_UNROLL = 8

PA_W = A_HEADS * HEAD_DIM + A_KV_HEADS * HEAD_DIM
PA_V = A_KV_HEADS * LANES
PB_W = B_HEADS * HEAD_DIM + 2 * B_KV_HEADS * HEAD_DIM
PC_W = C_Q_LORA + C_KV_LORA + LANES
OFF_A, OFF_AV = 0, PA_W
OFF_B = OFF_AV + PA_V
OFF_C = OFF_B + PB_W
P1_COLS = OFF_C + PC_W


def _cparams(semantics):
    return pltpu.CompilerParams(dimension_semantics=semantics, vmem_limit_bytes=VMEM_LIMIT_BYTES)


def _layer_norm(x, g, b):
    mu = jnp.mean(x, axis=-1, keepdims=True)
    xc = x - mu
    var = jnp.mean(xc * xc, axis=-1, keepdims=True)
    return xc * lax.rsqrt(var + NORM_EPS) * g + b


def _swap16(x):
    lane = lax.broadcasted_iota(jnp.int32, x.shape, x.ndim - 1)
    up = pltpu.roll(x, LANES - 16, axis=x.ndim - 1)
    down = pltpu.roll(x, 16, axis=x.ndim - 1)
    return jnp.where((lane & 16) == 0, up, down)


def _rope_tiles(x, cos, sin):
    tiles = []
    for c in range(x.shape[-1] // LANES):
        xc = x[:, c * LANES:(c + 1) * LANES]
        tiles.append(xc * cos + _swap16(xc) * sin)
    return jnp.concatenate(tiles, axis=-1) if len(tiles) > 1 else tiles[0]


def _store_vt(vt_ref, v_ext, kv_heads):
    vt = v_ext.T
    for kv in range(kv_heads):
        vt_ref[kv * VT_ROWS:(kv + 1) * VT_ROWS, :] = vt[kv * LANES:kv * LANES + VT_ROWS].astype(BF16)


def _proj_kernel(x_ref, lng_ref, lnb_ref, w1_ref, bd_ref, ga_ref, cosa_ref, sina_ref,
                 gcq_ref, gckv_ref, wuq_ref, wukv_ref, cosq_ref, sinq_ref, cosk_ref, sink_ref,
                 vones_a_ref, vones_c_ref,
                 *out_refs, apply_ln):
    if apply_ln:
        xn_ref, qa_ref, ka_ref, va_ref, qb_ref, kb_ref, vb_ref, qc_ref, kc_ref, vc_ref = out_refs
    else:
        qa_ref, ka_ref, va_ref, qb_ref, kb_ref, vb_ref, qc_ref, kc_ref, vc_ref = out_refs
    x = x_ref[...]
    if apply_ln:
        x = _layer_norm(x, lng_ref[...], lnb_ref[...])
        xn_ref[...] = x
    xb = x.astype(BF16)
    p = jnp.dot(xb, w1_ref[...], preferred_element_type=F32)

    qk = p[:, OFF_A:OFF_A + PA_W]
    ms = jnp.dot((qk * qk).astype(BF16), bd_ref[...], preferred_element_type=F32)
    qk = qk * lax.rsqrt(ms + NORM_EPS) * ga_ref[...]
    qk = _rope_tiles(qk, cosa_ref[...], sina_ref[...])
    nq = A_HEADS * HEAD_DIM
    qa_ref[...] = qk[:, :nq].astype(BF16)
    kk = qk[:, nq:]
    low = lax.broadcasted_iota(jnp.int32, kk.shape, 1) < HEAD_DIM
    ka_ref[...] = jnp.concatenate([jnp.where(low, kk, 0.0),
                                   jnp.where(low, pltpu.roll(kk, HEAD_DIM, axis=1), 0.0)],
                                  axis=-1).astype(BF16)
    _store_vt(va_ref, p[:, OFF_AV:OFF_AV + PA_V] + vones_a_ref[...], A_KV_HEADS)

    nqb = B_HEADS * HEAD_DIM
    nkb = B_KV_HEADS * HEAD_DIM
    qb_ref[...] = p[:, OFF_B:OFF_B + nqb].astype(BF16)
    kb_ref[...] = p[:, OFF_B + nqb:OFF_B + nqb + nkb].astype(BF16)
    vb_ref[...] = p[:, OFF_B + nqb + nkb:OFF_B + PB_W].astype(BF16)

    cq = p[:, OFF_C:OFF_C + C_Q_LORA]
    ckv = p[:, OFF_C + C_Q_LORA:OFF_C + C_Q_LORA + C_KV_LORA]
    kr = p[:, OFF_C + C_Q_LORA + C_KV_LORA:OFF_C + PC_W]
    cqn = cq * lax.rsqrt(jnp.mean(cq * cq, axis=-1, keepdims=True) + NORM_EPS) * gcq_ref[...]
    ckvn = ckv * lax.rsqrt(jnp.mean(ckv * ckv, axis=-1, keepdims=True) + NORM_EPS) * gckv_ref[...]
    qc = jnp.dot(cqn.astype(BF16), wuq_ref[...], preferred_element_type=F32)
    qc_ref[...] = _rope_tiles(qc, cosq_ref[...], sinq_ref[...]).astype(BF16)
    kv = jnp.dot(ckvn.astype(BF16), wukv_ref[...], preferred_element_type=F32)
    kpe = _rope_tiles(kr, cosk_ref[...], sink_ref[...])
    hw = C_HEADS * LANES
    kc_ref[...] = (kv[:, :hw] + jnp.concatenate([kpe] * C_HEADS, axis=-1)).astype(BF16)
    _store_vt(vc_ref, kv[:, hw:] + vones_c_ref[...], C_HEADS)


def _proj_call(x, lng, lnb, w1, bd, ga, cosa, sina, gcq, gckv, wuq, wukv, cosq, sinq, cosk, sink,
               vones_a, vones_c, *, seq, apply_ln):
    n = x.shape[0]
    tm = PROJ_ROWS
    nsb = seq // tm
    row = lambda w: pl.BlockSpec((tm, w), lambda i: (i, 0))
    full = lambda a: pl.BlockSpec(a.shape, lambda i: (0,) * a.ndim, pipeline_mode=pl.Buffered(1))
    tab = pl.BlockSpec((tm, LANES), lambda i: (i % nsb, 0))
    out_w = [A_HEADS * HEAD_DIM, A_KV_HEADS * LANES, -A_KV_HEADS * VT_ROWS,
             B_HEADS * HEAD_DIM, B_KV_HEADS * HEAD_DIM, B_KV_HEADS * HEAD_DIM,
             C_HEADS * LANES, C_HEADS * LANES, -C_HEADS * VT_ROWS]
    out_shape = [jax.ShapeDtypeStruct((n, w) if w > 0 else (-w, n), BF16) for w in out_w]
    out_specs = [row(w) if w > 0 else pl.BlockSpec((-w, tm), lambda i: (0, i)) for w in out_w]
    if apply_ln:
        out_shape = [jax.ShapeDtypeStruct((n, D_MODEL), F32)] + out_shape
        out_specs = [row(D_MODEL)] + out_specs
    return pl.pallas_call(
        functools.partial(_proj_kernel, apply_ln=apply_ln),
        out_shape=out_shape,
        grid=(n // tm,),
        in_specs=[row(D_MODEL), full(lng), full(lnb), full(w1), full(bd), full(ga), tab, tab,
                  full(gcq), full(gckv), full(wuq), full(wukv), tab, tab, tab, tab,
                  full(vones_a), full(vones_c)],
        out_specs=out_specs,
        compiler_params=_cparams(("parallel",)),
        name="proj_ln" if apply_ln else "proj",
    )(x, lng, lnb, w1, bd, ga, cosa, sina, gcq, gckv, wuq, wukv, cosq, sinq, cosk, sink,
      vones_a, vones_c)


def _dense_attn_kernel(q_ref, k_ref, vt_ref, o_ref, *, heads, kv_heads, dq):
    group = heads // kv_heads
    seq = k_ref.shape[0]
    tq = q_ref.shape[0]
    n_chunks = seq // ATTN_KEY_CHUNK
    units = [(h0 + dh, c) for h0 in range(0, heads, ATTN_HEADS_INTERLEAVED) for c in range(n_chunks)
             for dh in range(ATTN_HEADS_INTERLEAVED)]

    def scores_t(h, c):
        kv = h // group
        qh = q_ref[:, h * dq:(h + 1) * dq]
        kc = k_ref[c * ATTN_KEY_CHUNK:(c + 1) * ATTN_KEY_CHUNK, kv * LANES:(kv + 1) * LANES]
        if dq < LANES:
            kc = kc[:, :dq]
        return lax.dot_general(kc, qh, (((1,), (1,)), ((), ())), preferred_element_type=F32)

    ahead = 2
    st = {u: scores_t(*u) for u in units[:ahead]}
    outs = [None] * heads
    m, acc = {}, {}
    for idx, (h, c) in enumerate(units):
        kv = h // group
        s = st.pop((h, c))
        if idx + ahead < len(units):
            st[units[idx + ahead]] = scores_t(*units[idx + ahead])
        cmax = jnp.max(s, axis=0, keepdims=True)
        m_new = cmax if c == 0 else jnp.maximum(m[h], cmax)
        pt = jnp.exp2(s - m_new).astype(BF16)
        vtc = vt_ref[kv * VT_ROWS:(kv + 1) * VT_ROWS, c * ATTN_KEY_CHUNK:(c + 1) * ATTN_KEY_CHUNK]
        pv = jnp.dot(vtc, pt, preferred_element_type=F32)
        acc[h] = pv if c == 0 else acc[h] * jnp.exp2(m[h] - m_new) + pv
        m[h] = m_new
        if c == n_chunks - 1:
            outs[h] = acc[h][:HEAD_DIM] / acc[h][HEAD_DIM:HEAD_DIM + 1]
    o_ref[...] = jnp.concatenate(outs, axis=0).T.astype(o_ref.dtype)


def _dense_attn_call(q, k, vt, *, batch, seq, heads, kv_heads, dq, name):
    n = q.shape[0]
    tq = ATTN_Q_ROWS
    nq = seq // tq
    return pl.pallas_call(
        functools.partial(_dense_attn_kernel, heads=heads, kv_heads=kv_heads, dq=dq),
        out_shape=jax.ShapeDtypeStruct((n, heads * HEAD_DIM), BF16),
        grid=(batch, nq),
        in_specs=[pl.BlockSpec((tq, heads * dq), lambda b, i: (b * nq + i, 0)),
                  pl.BlockSpec((seq, kv_heads * LANES), lambda b, i: (b, 0)),
                  pl.BlockSpec((kv_heads * VT_ROWS, seq), lambda b, i: (0, b))],
        out_specs=pl.BlockSpec((tq, heads * HEAD_DIM), lambda b, i: (b * nq + i, 0)),
        compiler_params=_cparams(("parallel", "parallel")),
        name=name,
    )(q, k, vt)


def _win_attn_kernel(q_ref, k_ref, v_ref, bias_ref, o_ref, *, heads, kv_heads, nb):
    group = heads // kv_heads
    tq = WIN_ROWS
    for j in range(WIN_BLOCKS_PER_STEP):
        i = pl.program_id(1) * WIN_BLOCKS_PER_STEP + j
        outs = [None] * heads
        for kv in range(kv_heads):
            qs = jnp.concatenate(
                [q_ref[j * tq:(j + 1) * tq, (kv * group + g) * HEAD_DIM:(kv * group + g + 1) * HEAD_DIM]
                 for g in range(group)], axis=0)
            bias = bias_ref[kv]
            logit_chunks, v_chunks = [], []
            for c in range(3):
                blk = i + (c - 1)
                start = pl.multiple_of(jnp.clip(blk, 0, nb - 1) * tq, tq)
                kc = k_ref[pl.ds(start, tq), kv * HEAD_DIM:(kv + 1) * HEAD_DIM]
                v_chunks.append(v_ref[pl.ds(start, tq), kv * HEAD_DIM:(kv + 1) * HEAD_DIM])
                sc = lax.dot_general(qs, kc, (((1,), (1,)), ((), ())), preferred_element_type=F32)
                sc = sc + bias[:, c * tq:(c + 1) * tq]
                if c != 1:
                    sc = jnp.where(jnp.logical_and(blk >= 0, blk <= nb - 1), sc, NEG_INF)
                logit_chunks.append(sc)
            logit_chunks.append(bias[:, 3 * tq:])
            logits = jnp.concatenate(logit_chunks, axis=-1)
            m = jnp.max(logits, axis=-1, keepdims=True)
            pr = jnp.exp2(logits - m)
            denom = jnp.sum(pr, axis=-1, keepdims=True)
            vv = jnp.concatenate(v_chunks, axis=0)
            o = jnp.dot(pr[:, :3 * tq].astype(BF16), vv, preferred_element_type=F32) / denom
            for g in range(group):
                outs[kv * group + g] = o[g * tq:(g + 1) * tq]
        o_ref[j * tq:(j + 1) * tq, :] = jnp.concatenate(outs, axis=-1).astype(o_ref.dtype)


def _win_attn_call(q, k, v, bias, *, batch, seq, heads, kv_heads):
    n = q.shape[0]
    ts = WIN_ROWS * WIN_BLOCKS_PER_STEP
    nq = seq // ts
    return pl.pallas_call(
        functools.partial(_win_attn_kernel, heads=heads, kv_heads=kv_heads, nb=seq // WIN_ROWS),
        out_shape=jax.ShapeDtypeStruct((n, heads * HEAD_DIM), BF16),
        grid=(batch, nq),
        in_specs=[pl.BlockSpec((ts, heads * HEAD_DIM), lambda b, i: (b * nq + i, 0)),
                  pl.BlockSpec((seq, kv_heads * HEAD_DIM), lambda b, i: (b, 0)),
                  pl.BlockSpec((seq, kv_heads * HEAD_DIM), lambda b, i: (b, 0)),
                  pl.BlockSpec(bias.shape, lambda b, i: (0, 0, 0))],
        out_specs=pl.BlockSpec((ts, heads * HEAD_DIM), lambda b, i: (b * nq + i, 0)),
        compiler_params=_cparams(("parallel", "parallel")),
        name="win_attn",
    )(q, k, v, bias)


def _pack_rows(x):
    half = x.shape[-1] // 2
    hi = pltpu.bitcast(x[:, :half].astype(BF16).astype(F32), jnp.uint32)
    lo = pltpu.bitcast(x[:, half:].astype(BF16).astype(F32), jnp.uint32)
    return hi | (lo >> 16)


def _unpack_rows(u):
    hi = pltpu.bitcast(u & jnp.uint32(0xFFFF0000), F32)
    lo = pltpu.bitcast(u << 16, F32)
    return jnp.concatenate([hi, lo], axis=-1)


def _store_row_tiles(ref, u):
    rows = u.shape[0]
    for c in range(ROW_TILES):
        ref[pl.ds(c, rows, stride=ROW_TILES), :] = u[:, c * LANES:(c + 1) * LANES]


def _load_row_tiles(ref):
    rows = ref.shape[0] // ROW_TILES
    return jnp.concatenate([ref[pl.ds(c, rows, stride=ROW_TILES), :] for c in range(ROW_TILES)], axis=-1)


def _merge_kernel(x_ref, oa_ref, ob_ref, oc_ref, wg_ref, wa_ref, wb_ref, wc_ref, wo_ref,
                  g_ref, b_ref, rwt_ref, rb_ref, tri_ref,
                  x1_ref, xp_ref, e_ref, w_ref, r_ref, cnt_ref, carry_ref, *, alpha):
    step = pl.program_id(0)

    @pl.when(step == 0)
    def _():
        carry_ref[...] = jnp.zeros_like(carry_ref)

    x = x_ref[...]
    xb = x.astype(BF16)
    d = D_MODEL
    mixers = ((oa_ref[...], wa_ref), (ob_ref[...], wb_ref), (oc_ref[...], wc_ref))
    tiles = []
    for c0 in range(0, d, MERGE_COL_TILE):
        acc = None
        for idx, (o, wbr_ref) in enumerate(mixers):
            gate = jax.nn.sigmoid(jnp.dot(xb, wg_ref[:, idx * d + c0:idx * d + c0 + MERGE_COL_TILE],
                                          preferred_element_type=F32))
            br = jnp.dot(o, wbr_ref[:, c0:c0 + MERGE_COL_TILE], preferred_element_type=F32)
            acc = gate * br if acc is None else acc + gate * br
        tiles.append(acc.astype(BF16))
    merged = jnp.concatenate(tiles, axis=-1)
    mix = jnp.dot(merged, wo_ref[...], preferred_element_type=F32)
    x1 = _layer_norm(alpha * x + mix, g_ref[...], b_ref[...])
    x1_ref[...] = x1
    _store_row_tiles(xp_ref, _pack_rows(x1))

    logits = lax.dot_general(rwt_ref[...], x1, (((1,), (1,)), ((), ())),
                             preferred_element_type=F32, precision=lax.Precision.HIGHEST)
    scores = jax.nn.sigmoid(logits)
    biased = scores + rb_ref[...]
    shape = biased.shape
    rows = lax.broadcasted_iota(jnp.int32, shape, 0)
    sub = lax.broadcasted_iota(jnp.int32, (EXPERTS_PER_GROUP, shape[1]), 0)
    best, sel = None, None
    for g in range(N_GROUPS):
        blk = biased[g * EXPERTS_PER_GROUP:(g + 1) * EXPERTS_PER_GROUP]
        m1 = jnp.max(blk, axis=0, keepdims=True)
        i1 = jnp.min(jnp.where(blk == m1, sub, EXPERTS_PER_GROUP), axis=0, keepdims=True)
        m2 = jnp.max(jnp.where(sub == i1, -jnp.inf, blk), axis=0, keepdims=True)
        gs = m1 + m2
        if best is None:
            best, sel = gs, jnp.zeros_like(i1)
        else:
            better = gs > best
            sel = jnp.where(better, g, sel)
            best = jnp.where(better, gs, best)
    in_group = (rows >> 2) == sel
    bm = jnp.where(in_group, biased, -jnp.inf)
    t1 = jnp.max(bm, axis=0, keepdims=True)
    e0 = jnp.min(jnp.where(bm == t1, rows, N_EXPERTS), axis=0, keepdims=True)
    bm2 = jnp.where(rows == e0, -jnp.inf, bm)
    t2 = jnp.max(bm2, axis=0, keepdims=True)
    e1 = jnp.min(jnp.where(bm2 == t2, rows, N_EXPERTS), axis=0, keepdims=True)
    hot0 = rows == e0
    hot1 = rows == e1
    s0 = jnp.sum(jnp.where(hot0, scores, 0.0), axis=0, keepdims=True)
    s1 = jnp.sum(jnp.where(hot1, scores, 0.0), axis=0, keepdims=True)
    tot = s0 + s1
    e_ref[...] = jnp.concatenate([e0, e1], axis=0)
    w_ref[...] = jnp.concatenate([s0 / tot, s1 / tot], axis=0)

    hot = jnp.where(jnp.logical_or(hot0, hot1), 1.0, 0.0)
    prefix = jnp.dot(hot.astype(BF16), tri_ref[...], preferred_element_type=F32) + carry_ref[:, 0:1]
    r0 = jnp.sum(jnp.where(hot0, prefix, 0.0), axis=0, keepdims=True)
    r1 = jnp.sum(jnp.where(hot1, prefix, 0.0), axis=0, keepdims=True)
    r_ref[...] = jnp.concatenate([r0, r1], axis=0).astype(jnp.int32)
    carry_ref[...] = carry_ref[...] + jnp.sum(hot, axis=1, keepdims=True)
    cnt_ref[...] = carry_ref[...]


def _merge_call(x, oa, ob, oc, wg, wa, wb, wc, wo, g, b, rwt, rb, tri, *, alpha):
    n = x.shape[0]
    tm = MERGE_ROWS
    row = lambda w: pl.BlockSpec((tm, w), lambda i: (i, 0))
    full = lambda a: pl.BlockSpec(a.shape, lambda i: (0,) * a.ndim, pipeline_mode=pl.Buffered(1))
    tok = pl.BlockSpec((TOP_K, tm), lambda i: (0, i))
    packed = pl.BlockSpec((tm * ROW_TILES, LANES), lambda i: (i, 0))
    return pl.pallas_call(
        functools.partial(_merge_kernel, alpha=alpha),
        out_shape=[jax.ShapeDtypeStruct((n, D_MODEL), F32),
                   jax.ShapeDtypeStruct((n * ROW_TILES, LANES), jnp.uint32),
                   jax.ShapeDtypeStruct((TOP_K, n), jnp.int32),
                   jax.ShapeDtypeStruct((TOP_K, n), F32),
                   jax.ShapeDtypeStruct((TOP_K, n), jnp.int32),
                   jax.ShapeDtypeStruct((N_EXPERTS, LANES), F32)],
        grid=(n // tm,),
        in_specs=[row(D_MODEL), row(oa.shape[1]), row(ob.shape[1]), row(oc.shape[1]),
                  full(wg), full(wa), full(wb), full(wc), full(wo), full(g), full(b),
                  full(rwt), full(rb), full(tri)],
        out_specs=[row(D_MODEL), packed, tok, tok, tok,
                   pl.BlockSpec((N_EXPERTS, LANES), lambda i: (0, 0))],
        scratch_shapes=[pltpu.VMEM((N_EXPERTS, LANES), F32)],
        compiler_params=_cparams(("arbitrary",)),
        name="merge",
    )(x, oa, ob, oc, wg, wa, wb, wc, wo, g, b, rwt, rb, tri)


def _row_slab(ref, first):
    if not isinstance(first, int):
        first = pl.multiple_of(first, ROW_TILES)
    return ref.at[pl.ds(first, ROW_TILES)]


def _dispatch_kernel(pos_ref, x_ref, dst_in_hbm, dst_hbm, sem):
    del dst_in_hbm
    tb = DISPATCH_ROWS

    def start(g, carry):
        for u in range(DMA_UNROLL):
            t = g * DMA_UNROLL + u
            for j in range(TOP_K):
                pltpu.make_async_copy(_row_slab(x_ref, t * ROW_TILES),
                                      _row_slab(dst_hbm, pos_ref[0, 0, j * tb + t]),
                                      sem).start(priority=(u + j) % 2)
        return carry

    lax.fori_loop(0, tb // DMA_UNROLL, start, 0)
    for j in range(TOP_K):
        pltpu.make_async_copy(x_ref, dst_hbm.at[pl.ds(0, tb * ROW_TILES)], sem).wait()


def _dispatch_call(pos_flat, xp, dst0):
    nsteps = pos_flat.shape[0]
    tb = DISPATCH_ROWS
    return pl.pallas_call(
        _dispatch_kernel,
        out_shape=jax.ShapeDtypeStruct(dst0.shape, dst0.dtype),
        grid=(nsteps,),
        in_specs=[pl.BlockSpec((1, 1, TOP_K * tb), lambda i: (i, 0, 0), memory_space=pltpu.SMEM),
                  pl.BlockSpec((tb * ROW_TILES, LANES), lambda i: (i, 0)),
                  pl.BlockSpec(memory_space=pl.ANY)],
        out_specs=pl.BlockSpec(memory_space=pl.ANY),
        scratch_shapes=[pltpu.SemaphoreType.DMA(())],
        input_output_aliases={2: 0},
        compiler_params=pltpu.CompilerParams(dimension_semantics=("arbitrary",), has_side_effects=True,
                                             vmem_limit_bytes=VMEM_LIMIT_BYTES),
        name="dispatch",
    )(pos_flat, xp, dst0)


def _expert_kernel(te_ref, nu_ref, xs_ref, wg_ref, wu_ref, wd_ref, ys_ref, wgu_bf, wd_bf):
    i = pl.program_id(0)

    @pl.when(jnp.logical_or(i == 0, te_ref[i] != te_ref[jnp.maximum(i - 1, 0)]))
    def _():
        wgu_bf[:, :D_FF] = wg_ref[0, 0].astype(BF16)
        wgu_bf[:, D_FF:] = wu_ref[0, 0].astype(BF16)
        wd_bf[...] = wd_ref[0, 0].astype(BF16)

    @pl.when(i < nu_ref[0])
    def _():
        xb = _unpack_rows(_load_row_tiles(xs_ref)).astype(BF16)
        hid = []
        for c0 in range(0, D_FF, EXPERT_COL_TILE):
            gate = jnp.dot(xb, wgu_bf[:, c0:c0 + EXPERT_COL_TILE], preferred_element_type=F32)
            up = jnp.dot(xb, wgu_bf[:, D_FF + c0:D_FF + c0 + EXPERT_COL_TILE], preferred_element_type=F32)
            hid.append(((gate * jax.nn.sigmoid(gate)) * up).astype(BF16))
        y = jnp.dot(jnp.concatenate(hid, axis=-1), wd_bf[...], preferred_element_type=F32)
        _store_row_tiles(ys_ref, _pack_rows(y))

    @pl.when(i >= nu_ref[0])
    def _():
        ys_ref[...] = jnp.zeros_like(ys_ref)


def _expert_call(tile_expert, n_used, xs, w_gate, w_up, w_down, *, layer):
    p = xs.shape[0] // ROW_TILES
    tm = EXPERT_ROWS
    wspec = lambda r, c: pl.BlockSpec((1, 1, r, c), lambda i, te, nu: (layer, te[i], 0, 0))
    grid_spec = pltpu.PrefetchScalarGridSpec(
        num_scalar_prefetch=2,
        grid=(p // tm,),
        in_specs=[pl.BlockSpec((tm * ROW_TILES, LANES), lambda i, te, nu: (i, 0)),
                  wspec(D_MODEL, D_FF), wspec(D_MODEL, D_FF), wspec(D_FF, D_MODEL)],
        out_specs=pl.BlockSpec((tm * ROW_TILES, LANES), lambda i, te, nu: (i, 0)),
        scratch_shapes=[pltpu.VMEM((D_MODEL, 2 * D_FF), BF16), pltpu.VMEM((D_FF, D_MODEL), BF16)],
    )
    return pl.pallas_call(
        _expert_kernel,
        out_shape=jax.ShapeDtypeStruct((p * ROW_TILES, LANES), jnp.uint32),
        grid_spec=grid_spec,
        compiler_params=_cparams(("arbitrary",)),
        name="experts",
    )(tile_expert, n_used, xs, w_gate, w_up, w_down)


def _combine_kernel(pos_ref, nxt_ref, ys_hbm, x_ref, w_ref, g_ref, b_ref, o_ref, buf_ref, sem, *, alpha):
    tc = COMBINE_ROWS
    i = pl.program_id(0)
    slot = i % 2

    def row_copy(idx_ref, to_slot, t, j):
        return pltpu.make_async_copy(_row_slab(ys_hbm, idx_ref[0, 0, j * tc + t]),
                                     _row_slab(buf_ref.at[to_slot, j], t * ROW_TILES), sem.at[to_slot])

    def wait_slot(s):
        for j in range(TOP_K):
            pltpu.make_async_copy(ys_hbm.at[pl.ds(0, tc * ROW_TILES)], buf_ref.at[s, j], sem.at[s]).wait()

    @pl.when(i == 0)
    def _():
        def start(g, carry):
            for u in range(DMA_UNROLL):
                for j in range(TOP_K):
                    row_copy(pos_ref, 0, g * DMA_UNROLL + u, j).start(priority=(u + j) % 2)
            return carry
        lax.fori_loop(0, tc // DMA_UNROLL, start, 0)

    for t in range(tc):
        for j in range(TOP_K):
            row_copy(nxt_ref, 1 - slot, t, j).start(priority=(t + j) % 2)
    wait_slot(slot)
    w = w_ref[...]
    ffn = (w[:, 0:1] * _unpack_rows(_load_row_tiles(buf_ref.at[slot, 0]))
           + w[:, 1:2] * _unpack_rows(_load_row_tiles(buf_ref.at[slot, 1])))
    o_ref[...] = _layer_norm(alpha * x_ref[...] + ffn, g_ref[...], b_ref[...])

    @pl.when(i == pl.num_programs(0) - 1)
    def _():
        wait_slot(1 - slot)


def _combine_call(pos_flat, ys, x1, wts_t, g, b, *, alpha):
    n = x1.shape[0]
    tc = COMBINE_ROWS
    nsteps = n // tc
    idx = lambda shift: pl.BlockSpec((1, 1, TOP_K * tc), lambda i: (jnp.minimum(i + shift, nsteps - 1), 0, 0),
                                     memory_space=pltpu.SMEM)
    return pl.pallas_call(
        functools.partial(_combine_kernel, alpha=alpha),
        out_shape=jax.ShapeDtypeStruct((n, D_MODEL), F32),
        grid=(nsteps,),
        in_specs=[idx(0), idx(1),
                  pl.BlockSpec(memory_space=pl.ANY),
                  pl.BlockSpec((tc, D_MODEL), lambda i: (i, 0)),
                  pl.BlockSpec((tc, TOP_K), lambda i: (i, 0)),
                  pl.BlockSpec(g.shape, lambda i: (0, 0)),
                  pl.BlockSpec(b.shape, lambda i: (0, 0))],
        out_specs=pl.BlockSpec((tc, D_MODEL), lambda i: (i, 0)),
        scratch_shapes=[pltpu.VMEM((2, TOP_K, tc * ROW_TILES, LANES), jnp.uint32),
                        pltpu.SemaphoreType.DMA((2,))],
        compiler_params=_cparams(("arbitrary",)),
        name="combine",
    )(pos_flat, pos_flat, ys, x1, wts_t, g, b)


def _t5_bucket(rel):
    nb = N_BUCKETS // 2
    ret = (rel > 0).astype(np.int32) * nb
    n = np.abs(rel)
    max_exact = nb // 2
    large = max_exact + (np.log(np.maximum(n, 1) / max_exact)
                         / math.log(MAX_DISTANCE / max_exact) * (nb - max_exact)).astype(np.int32)
    large = np.minimum(large, nb - 1)
    return (ret + np.where(n < max_exact, n, large)).astype(np.int32)


def _rope_tables(seq):
    def cs(pos, dim):
        inv = ROPE_THETA ** (-(jnp.arange(0, dim, 2, dtype=F32) / dim))
        ang = pos.astype(F32)[:, None] * inv[None, :]
        return jnp.cos(ang), jnp.sin(ang)

    t = jnp.arange(seq)
    cr, sr = cs(t // GRID_W, HEAD_DIM // 2)
    cc, sc = cs(t % GRID_W, HEAD_DIM // 2)
    cos_head = jnp.concatenate([cr, cr, cc, cc], axis=-1)
    sin_head = jnp.concatenate([-sr, sr, -sc, sc], axis=-1)
    cos_a = jnp.concatenate([cos_head, cos_head], axis=-1)
    sin_a = jnp.concatenate([sin_head, sin_head], axis=-1)
    cs_, ss_ = cs(t, C_ROPE)
    ones = jnp.ones((seq, C_NOPE), F32)
    pad1 = jnp.ones((seq, LANES - C_NOPE - C_ROPE), F32)
    cos_k = jnp.concatenate([ones, cs_, cs_, pad1], axis=-1)
    sin_k = jnp.concatenate([0 * ones, -ss_, ss_, 0 * pad1], axis=-1)
    scale = (C_NOPE + C_ROPE) ** -0.5 * LOG2E
    return cos_a, sin_a, cos_k * scale, sin_k * scale, cos_k, sin_k


def _window_bias(rpb_table, b_sink_l):
    span = WIN_ROWS + 2 * WINDOW
    rel = np.arange(span)[None, :] - WINDOW - np.arange(WIN_ROWS)[:, None]
    band = jnp.asarray(np.abs(rel) <= WINDOW)
    onehot = jnp.asarray(_t5_bucket(rel)[:, :, None] == np.arange(N_BUCKETS), F32)
    bias = jnp.einsum("qsb,bh->qsh", onehot, rpb_table.astype(F32),
                      precision=lax.Precision.HIGHEST)
    bias = jnp.where(band[:, :, None], bias * LOG2E, NEG_INF)
    bias = jnp.transpose(bias, (2, 0, 1))
    sink = jnp.full((B_HEADS, WIN_ROWS, LANES), NEG_INF, F32)
    sink = sink.at[:, :, 0].set(jnp.broadcast_to(b_sink_l.astype(F32)[:, None] * LOG2E, (B_HEADS, WIN_ROWS)))
    ext = jnp.concatenate([bias, sink], axis=-1)
    group = B_HEADS // B_KV_HEADS
    return ext.reshape(B_KV_HEADS, group * WIN_ROWS, span + LANES)


def _pad_cols(w, groups, width, total):
    k = w.shape[0]
    w = w.reshape(k, groups, width)
    w = jnp.pad(w, ((0, 0), (0, 0), (0, total - width)))
    return w.reshape(k, groups * total)


def _layer_params(l, w_in, a_q_norm_g, a_k_norm_g, c_q_norm_g, c_kv_norm_g, c_w_uq, c_w_ukv,
                  w_branch_a, w_branch_b, w_branch_c, w_o):
    w = w_in[l]
    splits = np.cumsum([0, A_HEADS * HEAD_DIM, A_KV_HEADS * HEAD_DIM, A_KV_HEADS * HEAD_DIM,
                        B_HEADS * HEAD_DIM, B_KV_HEADS * HEAD_DIM, B_KV_HEADS * HEAD_DIM,
                        C_Q_LORA, C_KV_LORA, C_ROPE, D_MODEL, D_MODEL, D_MODEL])
    part = [w[:, splits[i]:splits[i + 1]] for i in range(12)]
    qa, ka, va, qb, kb, vb, cq, ckv, kr, ga, gb, gc = part
    va_e = _pad_cols(va, A_KV_HEADS, HEAD_DIM, LANES)
    kr_p = jnp.pad(kr, ((0, 0), (C_NOPE, LANES - C_NOPE - C_ROPE)))
    w1 = jnp.concatenate([qa, ka, va_e, qb * (HEAD_DIM ** -0.5 * LOG2E), kb, vb, cq, ckv, kr_p],
                         axis=1).astype(BF16)
    gate_a = jnp.concatenate([jnp.tile(a_q_norm_g[l], A_HEADS) * (HEAD_DIM ** -0.5 * LOG2E),
                              jnp.tile(a_k_norm_g[l], A_KV_HEADS)])[None, :].astype(F32)
    wuq = _pad_cols(c_w_uq[l], C_HEADS, C_NOPE + C_ROPE, LANES).astype(BF16)
    ukv = c_w_ukv[l].reshape(C_KV_LORA, C_HEADS, C_NOPE + C_V)
    wuk = jnp.pad(ukv[:, :, :C_NOPE], ((0, 0), (0, 0), (0, LANES - C_NOPE))).reshape(C_KV_LORA, C_HEADS * LANES)
    wuv = jnp.pad(ukv[:, :, C_NOPE:], ((0, 0), (0, 0), (0, LANES - C_V))).reshape(C_KV_LORA, C_HEADS * LANES)
    wukv = jnp.concatenate([wuk, wuv], axis=1).astype(BF16)
    wg = jnp.concatenate([ga, gb, gc], axis=1).astype(BF16)
    return dict(w1=w1, gate_a=gate_a, gcq=c_q_norm_g[l][None, :].astype(F32),
                gckv=c_kv_norm_g[l][None, :].astype(F32), wuq=wuq, wukv=wukv, wg=wg,
                wa=w_branch_a[l].astype(BF16), wb=w_branch_b[l].astype(BF16),
                wc=w_branch_c[l].astype(BF16), wo=w_o[l].astype(BF16))


def kernel(x, ln_in_g, ln_in_b, w_in, a_q_norm_g, a_k_norm_g, b_sink, rpb_table, c_q_norm_g, c_kv_norm_g,
           c_w_uq, c_w_ukv, w_branch_a, w_branch_b, w_branch_c, w_o, ln1_g, ln1_b, router_w, router_bias,
           w_gate, w_up, w_down, ln2_g, ln2_b):
    batch, seq, d = x.shape
    depth = w_in.shape[0]
    n = batch * seq
    alpha = (2 * depth) ** 0.25
    assert d == D_MODEL and seq % PROJ_ROWS == 0 and seq % ATTN_Q_ROWS == 0
    assert seq % (WIN_ROWS * WIN_BLOCKS_PER_STEP) == 0
    assert n % DISPATCH_ROWS == 0 and n % MERGE_ROWS == 0 and n % COMBINE_ROWS == 0

    cos_a, sin_a, cos_q, sin_q, cos_k, sin_k = _rope_tables(seq)
    seg = np.arange(PA_W) // HEAD_DIM
    bd = jnp.asarray((seg[:, None] == seg[None, :]).astype(np.float32) / HEAD_DIM, BF16)
    ones_pat = np.zeros((1, LANES), np.float32)
    ones_pat[0, HEAD_DIM:] = 1.0
    vones_a = jnp.asarray(np.tile(ones_pat, (1, A_KV_HEADS)))
    vones_c = jnp.asarray(np.tile(ones_pat, (1, C_HEADS)))
    tri = jnp.asarray(np.triu(np.ones((MERGE_ROWS, MERGE_ROWS), np.float32), k=1), BF16)
    rwt = router_w.astype(F32).T
    rb = router_bias.astype(F32)[:, None]
    lng = ln_in_g.astype(F32)[None, :]
    lnb = ln_in_b.astype(F32)[None, :]

    tme = EXPERT_ROWS
    n_tiles = (TOP_K * n) // tme + N_EXPERTS
    p_rows = n_tiles * tme

    h = x.reshape(n, d).astype(F32)
    for l in range(depth):
        prm = _layer_params(l, w_in, a_q_norm_g, a_k_norm_g, c_q_norm_g, c_kv_norm_g, c_w_uq, c_w_ukv,
                            w_branch_a, w_branch_b, w_branch_c, w_o)
        outs = _proj_call(h, lng, lnb, prm["w1"], bd, prm["gate_a"], cos_a, sin_a, prm["gcq"], prm["gckv"],
                          prm["wuq"], prm["wukv"], cos_q, sin_q, cos_k, sin_k, vones_a, vones_c,
                          seq=seq, apply_ln=(l == 0))
        if l == 0:
            h, *outs = outs
        qa, ka, va, qb, kb, vb, qc, kc, vc = outs
        oa = _dense_attn_call(qa, ka, va, batch=batch, seq=seq, heads=A_HEADS, kv_heads=A_KV_HEADS,
                              dq=HEAD_DIM, name="attn_a")
        bias = _window_bias(rpb_table, b_sink[l])
        ob = _win_attn_call(qb, kb, vb, bias, batch=batch, seq=seq, heads=B_HEADS, kv_heads=B_KV_HEADS)
        oc = _dense_attn_call(qc, kc, vc, batch=batch, seq=seq, heads=C_HEADS, kv_heads=C_HEADS,
                              dq=LANES, name="attn_c")
        x1, xp, e_idx, wts, rank, cnt = _merge_call(
            h, oa, ob, oc, prm["wg"], prm["wa"], prm["wb"], prm["wc"], prm["wo"],
            ln1_g[l].astype(F32)[None, :], ln1_b[l].astype(F32)[None, :], rwt, rb, tri, alpha=alpha)

        counts = cnt[:, 0].astype(jnp.int32)
        padded = ((counts + tme - 1) // tme) * tme
        ends = jnp.cumsum(padded)
        offs = ends - padded
        pos = rank
        for e in range(N_EXPERTS):
            pos = pos + jnp.where(e_idx == e, offs[e], 0)
        tile_start = jnp.arange(n_tiles, dtype=jnp.int32) * tme
        tile_expert = jnp.minimum(jnp.sum(tile_start[:, None] >= ends[None, :], axis=1), N_EXPERTS - 1)
        tile_expert = tile_expert.astype(jnp.int32)
        n_used = (ends[-1:] // tme).astype(jnp.int32)

        first_row = pos * ROW_TILES

        def per_step(rows):
            return first_row.reshape(TOP_K, n // rows, rows).transpose(1, 0, 2).reshape(n // rows, 1, TOP_K * rows)

        xs = _dispatch_call(per_step(DISPATCH_ROWS), xp, jnp.zeros((p_rows * ROW_TILES, LANES), jnp.uint32))
        ys = _expert_call(tile_expert, n_used, xs, w_gate.astype(F32), w_up.astype(F32), w_down.astype(F32),
                          layer=l)
        pos_c = per_step(COMBINE_ROWS)
        h = _combine_call(pos_c, ys, x1, wts.T, ln2_g[l].astype(F32)[None, :], ln2_b[l].astype(F32)[None, :],
                          alpha=alpha)
    return h.reshape(batch, seq, d).astype(x.dtype)
```

```python
import functools
import math

import numpy as np
import jax
import jax.numpy as jnp
from jax import lax
from jax.experimental import pallas as pl
from jax.experimental.pallas import tpu as pltpu

F32 = jnp.float32
BF16 = jnp.bfloat16

D_MODEL = 1024
GRID_W = 64
HEAD_DIM = 64
A_HEADS, A_KV_HEADS = 8, 2
B_HEADS, B_KV_HEADS = 8, 2
WINDOW = 128
C_HEADS = 8
C_NOPE, C_ROPE, C_V = 64, 32, 64
C_Q_LORA, C_KV_LORA = 384, 256
N_BUCKETS, MAX_DISTANCE = 32, 128
N_EXPERTS, N_GROUPS, TOP_K = 16, 4, 2
EXPERTS_PER_GROUP = N_EXPERTS // N_GROUPS
D_FF = 512
ROPE_THETA = 10000.0
NORM_EPS = 1e-6
NEG_INF = -1e30

LANES = 128
ROW_TILES = D_MODEL // LANES
BF16_SUBLANES = 16
VT_ROWS = HEAD_DIM + BF16_SUBLANES
LOG2E = math.log2(math.e)
VMEM_LIMIT_BYTES = 48 * 1024 * 1024

PROJ_ROWS = 512
ATTN_Q_ROWS = 512
ATTN_KEY_CHUNK = 256
ATTN_HEADS_INTERLEAVED = 1
WIN_ROWS = 128
WIN_BLOCKS_PER_STEP = 8
MERGE_ROWS = 512
MERGE_COL_TILE = 256
DISPATCH_ROWS = 512
EXPERT_ROWS = 512
EXPERT_COL_TILE = 256
COMBINE_ROWS = 256
DMA_UNROLL = 8

PA_W = A_HEADS * HEAD_DIM + A_KV_HEADS * HEAD_DIM
PA_V = A_KV_HEADS * LANES
PB_W = B_HEADS * HEAD_DIM + B_KV_HEADS * HEAD_DIM + B_KV_HEADS * LANES
PC_W = C_Q_LORA + C_KV_LORA + LANES
OFF_A, OFF_AV = 0, PA_W
OFF_B = OFF_AV + PA_V
OFF_C = OFF_B + PB_W
P1_COLS = OFF_C + PC_W


def _cparams(semantics):
    return pltpu.CompilerParams(dimension_semantics=semantics, vmem_limit_bytes=VMEM_LIMIT_BYTES)


def _layer_norm(x, g, b):
    mu = jnp.mean(x, axis=-1, keepdims=True)
    xc = x - mu
    var = jnp.mean(xc * xc, axis=-1, keepdims=True)
    return xc * lax.rsqrt(var + NORM_EPS) * g + b


def _swap16(x):
    lane = lax.broadcasted_iota(jnp.int32, x.shape, x.ndim - 1)
    up = pltpu.roll(x, LANES - 16, axis=x.ndim - 1)
    down = pltpu.roll(x, 16, axis=x.ndim - 1)
    return jnp.where((lane & 16) == 0, up, down)


def _rope_tiles(x, cos, sin):
    tiles = []
    for c in range(x.shape[-1] // LANES):
        xc = x[:, c * LANES:(c + 1) * LANES]
        tiles.append(xc * cos + _swap16(xc) * sin)
    return jnp.concatenate(tiles, axis=-1) if len(tiles) > 1 else tiles[0]


def _store_vt(vt_ref, v_ext, kv_heads, keep=VT_ROWS):
    vt = v_ext.T
    for kv in range(kv_heads):
        vt_ref[kv * keep:(kv + 1) * keep, :] = vt[kv * LANES:kv * LANES + keep].astype(BF16)


def _proj_kernel(x_ref, lng_ref, lnb_ref, w1_ref, bd_ref, ga_ref, cosa_ref, sina_ref,
                 gcq_ref, gckv_ref, wuq_ref, wukv_ref, cosq_ref, sinq_ref, cosk_ref, sink_ref,
                 vones_a_ref, vones_c_ref,
                 *out_refs, apply_ln):
    if apply_ln:
        xn_ref, qa_ref, ka_ref, va_ref, qb_ref, kb_ref, vb_ref, qc_ref, kc_ref, vc_ref = out_refs
    else:
        qa_ref, ka_ref, va_ref, qb_ref, kb_ref, vb_ref, qc_ref, kc_ref, vc_ref = out_refs
    x = x_ref[...]
    if apply_ln:
        x = _layer_norm(x, lng_ref[...], lnb_ref[...])
        xn_ref[...] = x
    xb = x.astype(BF16)
    pc = jnp.dot(xb, w1_ref[:, OFF_C:OFF_C + PC_W], preferred_element_type=F32)
    pa = jnp.dot(xb, w1_ref[:, OFF_A:OFF_B], preferred_element_type=F32)

    cq = pc[:, :C_Q_LORA]
    ckv = pc[:, C_Q_LORA:C_Q_LORA + C_KV_LORA]
    kr = pc[:, C_Q_LORA + C_KV_LORA:]
    cqn = cq * lax.rsqrt(jnp.mean(cq * cq, axis=-1, keepdims=True) + NORM_EPS) * gcq_ref[...]
    ckvn = ckv * lax.rsqrt(jnp.mean(ckv * ckv, axis=-1, keepdims=True) + NORM_EPS) * gckv_ref[...]
    qc = jnp.dot(cqn.astype(BF16), wuq_ref[...], preferred_element_type=F32)
    kv = jnp.dot(ckvn.astype(BF16), wukv_ref[...], preferred_element_type=F32)
    qk = pa[:, :PA_W]
    ms = jnp.dot((qk * qk).astype(BF16), bd_ref[...], preferred_element_type=F32)
    pb = jnp.dot(xb, w1_ref[:, OFF_B:OFF_C], preferred_element_type=F32)

    qc_ref[...] = _rope_tiles(qc, cosq_ref[...], sinq_ref[...]).astype(BF16)
    kpe = _rope_tiles(kr, cosk_ref[...], sink_ref[...])
    hw = C_HEADS * LANES
    kc_ref[...] = (kv[:, :hw] + jnp.concatenate([kpe] * C_HEADS, axis=-1)).astype(BF16)
    _store_vt(vc_ref, kv[:, hw:] + vones_c_ref[...], C_HEADS)

    qk = qk * lax.rsqrt(ms + NORM_EPS) * ga_ref[...]
    qk = _rope_tiles(qk, cosa_ref[...], sina_ref[...])
    nq = A_HEADS * HEAD_DIM
    qa_ref[...] = qk[:, :nq].astype(BF16)
    kk = qk[:, nq:]
    low = lax.broadcasted_iota(jnp.int32, kk.shape, 1) < HEAD_DIM
    ka_ref[...] = jnp.concatenate([jnp.where(low, kk, 0.0),
                                   jnp.where(low, pltpu.roll(kk, HEAD_DIM, axis=1), 0.0)],
                                  axis=-1).astype(BF16)
    _store_vt(va_ref, pa[:, PA_W:] + vones_a_ref[...], A_KV_HEADS)

    nqb = B_HEADS * HEAD_DIM
    nkb = B_KV_HEADS * HEAD_DIM
    qb_ref[...] = pb[:, :nqb].astype(BF16)
    kb_ref[...] = pb[:, nqb:nqb + nkb].astype(BF16)
    _store_vt(vb_ref, pb[:, nqb + nkb:], B_KV_HEADS, keep=HEAD_DIM)


def _proj_call(x, lng, lnb, w1, bd, ga, cosa, sina, gcq, gckv, wuq, wukv, cosq, sinq, cosk, sink,
               vones_a, vones_c, *, seq, apply_ln):
    n = x.shape[0]
    tm = PROJ_ROWS
    nsb = seq // tm
    row = lambda w: pl.BlockSpec((tm, w), lambda i: (i, 0))
    full = lambda a: pl.BlockSpec(a.shape, lambda i: (0,) * a.ndim, pipeline_mode=pl.Buffered(1))
    tab = pl.BlockSpec((tm, LANES), lambda i: (i % nsb, 0))
    out_w = [A_HEADS * HEAD_DIM, A_KV_HEADS * LANES, -A_KV_HEADS * VT_ROWS,
             B_HEADS * HEAD_DIM, B_KV_HEADS * HEAD_DIM, -B_KV_HEADS * HEAD_DIM,
             C_HEADS * LANES, C_HEADS * LANES, -C_HEADS * VT_ROWS]
    out_shape = [jax.ShapeDtypeStruct((n, w) if w > 0 else (-w, n), BF16) for w in out_w]
    out_specs = [row(w) if w > 0 else pl.BlockSpec((-w, tm), lambda i: (0, i)) for w in out_w]
    if apply_ln:
        out_shape = [jax.ShapeDtypeStruct((n, D_MODEL), F32)] + out_shape
        out_specs = [row(D_MODEL)] + out_specs
    return pl.pallas_call(
        functools.partial(_proj_kernel, apply_ln=apply_ln),
        out_shape=out_shape,
        grid=(n // tm,),
        in_specs=[row(D_MODEL), full(lng), full(lnb), full(w1), full(bd), full(ga), tab, tab,
                  full(gcq), full(gckv), full(wuq), full(wukv), tab, tab, tab, tab,
                  full(vones_a), full(vones_c)],
        out_specs=out_specs,
        compiler_params=_cparams(("parallel",)),
        name="proj_ln" if apply_ln else "proj",
    )(x, lng, lnb, w1, bd, ga, cosa, sina, gcq, gckv, wuq, wukv, cosq, sinq, cosk, sink,
      vones_a, vones_c)


def _dense_attn_kernel(q_ref, k_ref, vt_ref, o_ref, *, heads, kv_heads, dq):
    group = heads // kv_heads
    seq = k_ref.shape[0]
    tq = q_ref.shape[0]
    n_chunks = seq // ATTN_KEY_CHUNK
    units = [(h0 + dh, c) for h0 in range(0, heads, ATTN_HEADS_INTERLEAVED) for c in range(n_chunks)
             for dh in range(ATTN_HEADS_INTERLEAVED)]

    def scores_t(h, c):
        kv = h // group
        qh = q_ref[:, h * dq:(h + 1) * dq]
        kc = k_ref[c * ATTN_KEY_CHUNK:(c + 1) * ATTN_KEY_CHUNK, kv * LANES:(kv + 1) * LANES]
        if dq < LANES:
            kc = kc[:, :dq]
        return lax.dot_general(kc, qh, (((1,), (1,)), ((), ())), preferred_element_type=F32)

    ahead = 2
    st = {u: scores_t(*u) for u in units[:ahead]}
    outs = [None] * heads
    m, acc = {}, {}
    for idx, (h, c) in enumerate(units):
        kv = h // group
        s = st.pop((h, c))
        if idx + ahead < len(units):
            st[units[idx + ahead]] = scores_t(*units[idx + ahead])
        cmax = jnp.max(s, axis=0, keepdims=True)
        m_new = cmax if c == 0 else jnp.maximum(m[h], cmax)
        pt = jnp.exp2(s - m_new).astype(BF16)
        vtc = vt_ref[kv * VT_ROWS:(kv + 1) * VT_ROWS, c * ATTN_KEY_CHUNK:(c + 1) * ATTN_KEY_CHUNK]
        pv = jnp.dot(vtc, pt, preferred_element_type=F32)
        acc[h] = pv if c == 0 else acc[h] * jnp.exp2(m[h] - m_new) + pv
        m[h] = m_new
        if c == n_chunks - 1:
            outs[h] = acc[h][:HEAD_DIM] / acc[h][HEAD_DIM:HEAD_DIM + 1]
    o_ref[...] = jnp.concatenate(outs, axis=0).T.astype(o_ref.dtype)


def _dense_attn_call(q, k, vt, *, batch, seq, heads, kv_heads, dq, name):
    n = q.shape[0]
    tq = ATTN_Q_ROWS
    nq = seq // tq
    return pl.pallas_call(
        functools.partial(_dense_attn_kernel, heads=heads, kv_heads=kv_heads, dq=dq),
        out_shape=jax.ShapeDtypeStruct((n, heads * HEAD_DIM), BF16),
        grid=(batch, nq),
        in_specs=[pl.BlockSpec((tq, heads * dq), lambda b, i: (b * nq + i, 0)),
                  pl.BlockSpec((seq, kv_heads * LANES), lambda b, i: (b, 0)),
                  pl.BlockSpec((kv_heads * VT_ROWS, seq), lambda b, i: (0, b))],
        out_specs=pl.BlockSpec((tq, heads * HEAD_DIM), lambda b, i: (b * nq + i, 0)),
        compiler_params=_cparams(("parallel", "parallel")),
        name=name,
    )(q, k, vt)


def _win_attn_kernel(q_ref, k_ref, vt_ref, bias_ref, o_ref, *, heads, kv_heads, nb):
    group = heads // kv_heads
    tq = WIN_ROWS
    span = 3 * tq
    for j in range(WIN_BLOCKS_PER_STEP):
        i = pl.program_id(1) * WIN_BLOCKS_PER_STEP + j
        outs = []
        for kv in range(kv_heads):
            qs = jnp.concatenate(
                [q_ref[j * tq:(j + 1) * tq, (kv * group + g) * HEAD_DIM:(kv * group + g + 1) * HEAD_DIM]
                 for g in range(group)], axis=0)
            bias = bias_ref[kv]
            logit_chunks, vt_chunks = [], []
            for c in range(3):
                blk = i + (c - 1)
                start = pl.multiple_of(jnp.clip(blk, 0, nb - 1) * tq, tq)
                kc = k_ref[pl.ds(start, tq), kv * HEAD_DIM:(kv + 1) * HEAD_DIM]
                vt_chunks.append(vt_ref[kv * HEAD_DIM:(kv + 1) * HEAD_DIM, pl.ds(start, tq)])
                sc = lax.dot_general(kc, qs, (((1,), (1,)), ((), ())), preferred_element_type=F32)
                sc = sc + bias[c * tq:(c + 1) * tq]
                if c != 1:
                    sc = jnp.where(jnp.logical_and(blk >= 0, blk <= nb - 1), sc, NEG_INF)
                logit_chunks.append(sc)
            logit_chunks.append(bias[span:])
            logits = jnp.concatenate(logit_chunks, axis=0)
            m = jnp.max(logits, axis=0, keepdims=True)
            pr = jnp.exp2(logits - m)
            denom = jnp.sum(pr, axis=0, keepdims=True)
            vt = jnp.concatenate(vt_chunks, axis=1)
            ot = jnp.dot(vt, pr[:span].astype(BF16), preferred_element_type=F32) / denom
            outs += [ot[:, g * tq:(g + 1) * tq] for g in range(group)]
        o_ref[j * tq:(j + 1) * tq, :] = jnp.concatenate(outs, axis=0).T.astype(o_ref.dtype)


def _win_attn_call(q, k, vt, bias, *, batch, seq, heads, kv_heads):
    n = q.shape[0]
    ts = WIN_ROWS * WIN_BLOCKS_PER_STEP
    nq = seq // ts
    return pl.pallas_call(
        functools.partial(_win_attn_kernel, heads=heads, kv_heads=kv_heads, nb=seq // WIN_ROWS),
        out_shape=jax.ShapeDtypeStruct((n, heads * HEAD_DIM), BF16),
        grid=(batch, nq),
        in_specs=[pl.BlockSpec((ts, heads * HEAD_DIM), lambda b, i: (b * nq + i, 0)),
                  pl.BlockSpec((seq, kv_heads * HEAD_DIM), lambda b, i: (b, 0)),
                  pl.BlockSpec((kv_heads * HEAD_DIM, seq), lambda b, i: (0, b)),
                  pl.BlockSpec(bias.shape, lambda b, i: (0, 0, 0))],
        out_specs=pl.BlockSpec((ts, heads * HEAD_DIM), lambda b, i: (b * nq + i, 0)),
        compiler_params=_cparams(("parallel", "parallel")),
        name="win_attn",
    )(q, k, vt, bias)


def _store_row_tiles(ref, x):
    rows = x.shape[0]
    for c in range(ROW_TILES):
        ref[pl.ds(c, rows, stride=ROW_TILES), :] = x[:, c * LANES:(c + 1) * LANES]


def _load_row_tiles(ref):
    rows = ref.shape[0] // ROW_TILES
    return jnp.concatenate([ref[pl.ds(c, rows, stride=ROW_TILES), :] for c in range(ROW_TILES)], axis=-1)


def _merge_kernel(x_ref, oa_ref, ob_ref, oc_ref, wg_ref, wa_ref, wb_ref, wc_ref, wo_ref,
                  g_ref, b_ref, rwt_ref, rb_ref, tri_ref,
                  x1_ref, e_ref, w_ref, r_ref, cnt_ref, carry_ref, *, alpha):
    step = pl.program_id(0)

    @pl.when(step == 0)
    def _():
        carry_ref[...] = jnp.zeros_like(carry_ref)

    x = x_ref[...]
    xb = x.astype(BF16)
    d = D_MODEL
    mixers = ((oa_ref[...], wa_ref), (ob_ref[...], wb_ref), (oc_ref[...], wc_ref))
    tiles = []
    for c0 in range(0, d, MERGE_COL_TILE):
        acc = None
        for idx, (o, wbr_ref) in enumerate(mixers):
            gate = jax.nn.sigmoid(jnp.dot(xb, wg_ref[:, idx * d + c0:idx * d + c0 + MERGE_COL_TILE],
                                          preferred_element_type=F32))
            br = jnp.dot(o, wbr_ref[:, c0:c0 + MERGE_COL_TILE], preferred_element_type=F32)
            acc = gate * br if acc is None else acc + gate * br
        tiles.append(acc.astype(BF16))
    merged = jnp.concatenate(tiles, axis=-1)
    mix = jnp.dot(merged, wo_ref[...], preferred_element_type=F32)
    x1 = _layer_norm(alpha * x + mix, g_ref[...], b_ref[...])
    _store_row_tiles(x1_ref, x1)

    logits = lax.dot_general(rwt_ref[...], x1, (((1,), (1,)), ((), ())),
                             preferred_element_type=F32, precision=lax.Precision.HIGHEST)
    scores = jax.nn.sigmoid(logits)
    biased = scores + rb_ref[...]
    shape = biased.shape
    rows = lax.broadcasted_iota(jnp.int32, shape, 0)
    sub = lax.broadcasted_iota(jnp.int32, (EXPERTS_PER_GROUP, shape[1]), 0)
    best, sel = None, None
    for g in range(N_GROUPS):
        blk = biased[g * EXPERTS_PER_GROUP:(g + 1) * EXPERTS_PER_GROUP]
        m1 = jnp.max(blk, axis=0, keepdims=True)
        i1 = jnp.min(jnp.where(blk == m1, sub, EXPERTS_PER_GROUP), axis=0, keepdims=True)
        m2 = jnp.max(jnp.where(sub == i1, -jnp.inf, blk), axis=0, keepdims=True)
        gs = m1 + m2
        if best is None:
            best, sel = gs, jnp.zeros_like(i1)
        else:
            better = gs > best
            sel = jnp.where(better, g, sel)
            best = jnp.where(better, gs, best)
    in_group = (rows >> 2) == sel
    bm = jnp.where(in_group, biased, -jnp.inf)
    t1 = jnp.max(bm, axis=0, keepdims=True)
    e0 = jnp.min(jnp.where(bm == t1, rows, N_EXPERTS), axis=0, keepdims=True)
    bm2 = jnp.where(rows == e0, -jnp.inf, bm)
    t2 = jnp.max(bm2, axis=0, keepdims=True)
    e1 = jnp.min(jnp.where(bm2 == t2, rows, N_EXPERTS), axis=0, keepdims=True)
    hot0 = rows == e0
    hot1 = rows == e1
    s0 = jnp.sum(jnp.where(hot0, scores, 0.0), axis=0, keepdims=True)
    s1 = jnp.sum(jnp.where(hot1, scores, 0.0), axis=0, keepdims=True)
    tot = s0 + s1
    e_ref[...] = jnp.concatenate([e0, e1], axis=0)
    w_ref[...] = jnp.concatenate([s0 / tot, s1 / tot], axis=0)

    hot = jnp.where(jnp.logical_or(hot0, hot1), 1.0, 0.0)
    prefix = jnp.dot(hot.astype(BF16), tri_ref[...], preferred_element_type=F32) + carry_ref[:, 0:1]
    r0 = jnp.sum(jnp.where(hot0, prefix, 0.0), axis=0, keepdims=True)
    r1 = jnp.sum(jnp.where(hot1, prefix, 0.0), axis=0, keepdims=True)
    r_ref[...] = jnp.concatenate([r0, r1], axis=0).astype(jnp.int32)
    carry_ref[...] = carry_ref[...] + jnp.sum(hot, axis=1, keepdims=True)
    cnt_ref[...] = carry_ref[...]


def _merge_call(x, oa, ob, oc, wg, wa, wb, wc, wo, g, b, rwt, rb, tri, *, alpha):
    n = x.shape[0]
    tm = MERGE_ROWS
    row = lambda w: pl.BlockSpec((tm, w), lambda i: (i, 0))
    full = lambda a: pl.BlockSpec(a.shape, lambda i: (0,) * a.ndim, pipeline_mode=pl.Buffered(1))
    tok = pl.BlockSpec((TOP_K, tm), lambda i: (0, i))
    slabs = pl.BlockSpec((tm * ROW_TILES, LANES), lambda i: (i, 0))
    return pl.pallas_call(
        functools.partial(_merge_kernel, alpha=alpha),
        out_shape=[jax.ShapeDtypeStruct((n * ROW_TILES, LANES), F32),
                   jax.ShapeDtypeStruct((TOP_K, n), jnp.int32),
                   jax.ShapeDtypeStruct((TOP_K, n), F32),
                   jax.ShapeDtypeStruct((TOP_K, n), jnp.int32),
                   jax.ShapeDtypeStruct((N_EXPERTS, LANES), F32)],
        grid=(n // tm,),
        in_specs=[row(D_MODEL), row(oa.shape[1]), row(ob.shape[1]), row(oc.shape[1]),
                  full(wg), full(wa), full(wb), full(wc), full(wo), full(g), full(b),
                  full(rwt), full(rb), full(tri)],
        out_specs=[slabs, tok, tok, tok,
                   pl.BlockSpec((N_EXPERTS, LANES), lambda i: (0, 0))],
        scratch_shapes=[pltpu.VMEM((N_EXPERTS, LANES), F32)],
        compiler_params=_cparams(("arbitrary",)),
        name="merge",
    )(x, oa, ob, oc, wg, wa, wb, wc, wo, g, b, rwt, rb, tri)


def _row_slab(ref, first):
    if not isinstance(first, int):
        first = pl.multiple_of(first, ROW_TILES)
    return ref.at[pl.ds(first, ROW_TILES)]


def _dispatch_kernel(pos_ref, x_ref, dst_in_hbm, dst_hbm, sem):
    del dst_in_hbm
    tb = DISPATCH_ROWS

    def start(g, carry):
        for u in range(DMA_UNROLL):
            t = g * DMA_UNROLL + u
            for j in range(TOP_K):
                pltpu.make_async_copy(_row_slab(x_ref, t * ROW_TILES),
                                      _row_slab(dst_hbm, pos_ref[0, 0, j * tb + t]),
                                      sem).start(priority=(u + j) % 2)
        return carry

    lax.fori_loop(0, tb // DMA_UNROLL, start, 0)
    for j in range(TOP_K):
        pltpu.make_async_copy(x_ref, dst_hbm.at[pl.ds(0, tb * ROW_TILES)], sem).wait()


def _dispatch_call(pos_flat, xp, dst0):
    nsteps = pos_flat.shape[0]
    tb = DISPATCH_ROWS
    return pl.pallas_call(
        _dispatch_kernel,
        out_shape=jax.ShapeDtypeStruct(dst0.shape, dst0.dtype),
        grid=(nsteps,),
        in_specs=[pl.BlockSpec((1, 1, TOP_K * tb), lambda i: (i, 0, 0), memory_space=pltpu.SMEM),
                  pl.BlockSpec((tb * ROW_TILES, LANES), lambda i: (i, 0)),
                  pl.BlockSpec(memory_space=pl.ANY)],
        out_specs=pl.BlockSpec(memory_space=pl.ANY),
        scratch_shapes=[pltpu.SemaphoreType.DMA(())],
        input_output_aliases={2: 0},
        compiler_params=pltpu.CompilerParams(dimension_semantics=("arbitrary",), has_side_effects=True,
                                             vmem_limit_bytes=VMEM_LIMIT_BYTES),
        name="dispatch",
    )(pos_flat, xp, dst0)


def _expert_kernel(te_ref, nu_ref, xs_ref, wg_ref, wu_ref, wd_ref, ys_ref, wgu_bf, wd_bf):
    i = pl.program_id(0)

    @pl.when(jnp.logical_or(i == 0, te_ref[i] != te_ref[jnp.maximum(i - 1, 0)]))
    def _():
        wgu_bf[:, :D_FF] = wg_ref[0, 0].astype(BF16)
        wgu_bf[:, D_FF:] = wu_ref[0, 0].astype(BF16)
        wd_bf[...] = wd_ref[0, 0].astype(BF16)

    @pl.when(i < nu_ref[0])
    def _():
        xb = _load_row_tiles(xs_ref).astype(BF16)
        hid = []
        for c0 in range(0, D_FF, EXPERT_COL_TILE):
            gate = jnp.dot(xb, wgu_bf[:, c0:c0 + EXPERT_COL_TILE], preferred_element_type=F32)
            up = jnp.dot(xb, wgu_bf[:, D_FF + c0:D_FF + c0 + EXPERT_COL_TILE], preferred_element_type=F32)
            hid.append(((gate * jax.nn.sigmoid(gate)) * up).astype(BF16))
        y = jnp.dot(jnp.concatenate(hid, axis=-1), wd_bf[...], preferred_element_type=F32)
        _store_row_tiles(ys_ref, y)

    @pl.when(i >= nu_ref[0])
    def _():
        ys_ref[...] = jnp.zeros_like(ys_ref)


def _expert_call(tile_expert, n_used, xs, w_gate, w_up, w_down, *, layer):
    p = xs.shape[0] // ROW_TILES
    tm = EXPERT_ROWS
    wspec = lambda r, c: pl.BlockSpec((1, 1, r, c), lambda i, te, nu: (layer, te[i], 0, 0))
    grid_spec = pltpu.PrefetchScalarGridSpec(
        num_scalar_prefetch=2,
        grid=(p // tm,),
        in_specs=[pl.BlockSpec((tm * ROW_TILES, LANES), lambda i, te, nu: (i, 0)),
                  wspec(D_MODEL, D_FF), wspec(D_MODEL, D_FF), wspec(D_FF, D_MODEL)],
        out_specs=pl.BlockSpec((tm * ROW_TILES, LANES), lambda i, te, nu: (i, 0)),
        scratch_shapes=[pltpu.VMEM((D_MODEL, 2 * D_FF), BF16), pltpu.VMEM((D_FF, D_MODEL), BF16)],
    )
    return pl.pallas_call(
        _expert_kernel,
        out_shape=jax.ShapeDtypeStruct((p * ROW_TILES, LANES), F32),
        grid_spec=grid_spec,
        compiler_params=_cparams(("arbitrary",)),
        name="experts",
    )(tile_expert, n_used, xs, w_gate, w_up, w_down)


def _combine_kernel(pos_ref, nxt_ref, ys_hbm, x_ref, w_ref, g_ref, b_ref, o_ref, buf_ref, sem, *, alpha):
    tc = COMBINE_ROWS
    i = pl.program_id(0)
    slot = i % 2

    def row_copy(idx_ref, to_slot, t, j):
        return pltpu.make_async_copy(_row_slab(ys_hbm, idx_ref[0, 0, j * tc + t]),
                                     _row_slab(buf_ref.at[to_slot, j], t * ROW_TILES), sem.at[to_slot])

    def wait_slot(s):
        for j in range(TOP_K):
            pltpu.make_async_copy(ys_hbm.at[pl.ds(0, tc * ROW_TILES)], buf_ref.at[s, j], sem.at[s]).wait()

    @pl.when(i == 0)
    def _():
        def start(g, carry):
            for u in range(DMA_UNROLL):
                for j in range(TOP_K):
                    row_copy(pos_ref, 0, g * DMA_UNROLL + u, j).start(priority=(u + j) % 2)
            return carry
        lax.fori_loop(0, tc // DMA_UNROLL, start, 0)

    for t in range(tc):
        for j in range(TOP_K):
            row_copy(nxt_ref, 1 - slot, t, j).start(priority=(t + j) % 2)
    wait_slot(slot)
    w = w_ref[...]
    ffn = (w[:, 0:1] * _load_row_tiles(buf_ref.at[slot, 0])
           + w[:, 1:2] * _load_row_tiles(buf_ref.at[slot, 1]))
    o_ref[...] = _layer_norm(alpha * _load_row_tiles(x_ref) + ffn, g_ref[...], b_ref[...])

    @pl.when(i == pl.num_programs(0) - 1)
    def _():
        wait_slot(1 - slot)


def _combine_call(pos_flat, ys, x1, wts_t, g, b, *, alpha):
    n = x1.shape[0] // ROW_TILES
    tc = COMBINE_ROWS
    nsteps = n // tc
    idx = lambda shift: pl.BlockSpec((1, 1, TOP_K * tc), lambda i: (jnp.minimum(i + shift, nsteps - 1), 0, 0),
                                     memory_space=pltpu.SMEM)
    return pl.pallas_call(
        functools.partial(_combine_kernel, alpha=alpha),
        out_shape=jax.ShapeDtypeStruct((n, D_MODEL), F32),
        grid=(nsteps,),
        in_specs=[idx(0), idx(1),
                  pl.BlockSpec(memory_space=pl.ANY),
                  pl.BlockSpec((tc * ROW_TILES, LANES), lambda i: (i, 0)),
                  pl.BlockSpec((tc, TOP_K), lambda i: (i, 0)),
                  pl.BlockSpec(g.shape, lambda i: (0, 0)),
                  pl.BlockSpec(b.shape, lambda i: (0, 0))],
        out_specs=pl.BlockSpec((tc, D_MODEL), lambda i: (i, 0)),
        scratch_shapes=[pltpu.VMEM((2, TOP_K, tc * ROW_TILES, LANES), F32),
                        pltpu.SemaphoreType.DMA((2,))],
        compiler_params=_cparams(("arbitrary",)),
        name="combine",
    )(pos_flat, pos_flat, ys, x1, wts_t, g, b)


def _t5_bucket(rel):
    nb = N_BUCKETS // 2
    ret = (rel > 0).astype(np.int32) * nb
    n = np.abs(rel)
    max_exact = nb // 2
    large = max_exact + (np.log(np.maximum(n, 1) / max_exact)
                         / math.log(MAX_DISTANCE / max_exact) * (nb - max_exact)).astype(np.int32)
    large = np.minimum(large, nb - 1)
    return (ret + np.where(n < max_exact, n, large)).astype(np.int32)


def _rope_tables(seq):
    def cs(pos, dim):
        inv = ROPE_THETA ** (-(jnp.arange(0, dim, 2, dtype=F32) / dim))
        ang = pos.astype(F32)[:, None] * inv[None, :]
        return jnp.cos(ang), jnp.sin(ang)

    t = jnp.arange(seq)
    cr, sr = cs(t // GRID_W, HEAD_DIM // 2)
    cc, sc = cs(t % GRID_W, HEAD_DIM // 2)
    cos_head = jnp.concatenate([cr, cr, cc, cc], axis=-1)
    sin_head = jnp.concatenate([-sr, sr, -sc, sc], axis=-1)
    cos_a = jnp.concatenate([cos_head, cos_head], axis=-1)
    sin_a = jnp.concatenate([sin_head, sin_head], axis=-1)
    cs_, ss_ = cs(t, C_ROPE)
    ones = jnp.ones((seq, C_NOPE), F32)
    pad1 = jnp.ones((seq, LANES - C_NOPE - C_ROPE), F32)
    cos_k = jnp.concatenate([ones, cs_, cs_, pad1], axis=-1)
    sin_k = jnp.concatenate([0 * ones, -ss_, ss_, 0 * pad1], axis=-1)
    scale = (C_NOPE + C_ROPE) ** -0.5 * LOG2E
    return cos_a, sin_a, cos_k * scale, sin_k * scale, cos_k, sin_k


def _window_bias(rpb_table, b_sink_l):
    span = WIN_ROWS + 2 * WINDOW
    rel = np.arange(span)[None, :] - WINDOW - np.arange(WIN_ROWS)[:, None]
    band = jnp.asarray(np.abs(rel) <= WINDOW)
    onehot = jnp.asarray(_t5_bucket(rel)[:, :, None] == np.arange(N_BUCKETS), F32)
    bias = jnp.einsum("qsb,bh->qsh", onehot, rpb_table.astype(F32),
                      precision=lax.Precision.HIGHEST)
    bias = jnp.where(band[:, :, None], bias * LOG2E, NEG_INF)
    bias = jnp.transpose(bias, (1, 2, 0))
    sink = jnp.full((8, B_HEADS, WIN_ROWS), NEG_INF, F32)
    sink = sink.at[0].set(jnp.broadcast_to(b_sink_l.astype(F32)[:, None] * LOG2E, (B_HEADS, WIN_ROWS)))
    ext = jnp.concatenate([bias, sink], axis=0)
    group = B_HEADS // B_KV_HEADS
    ext = ext.reshape(span + 8, B_KV_HEADS, group * WIN_ROWS)
    return jnp.transpose(ext, (1, 0, 2))


def _pad_cols(w, groups, width, total):
    k = w.shape[0]
    w = w.reshape(k, groups, width)
    w = jnp.pad(w, ((0, 0), (0, 0), (0, total - width)))
    return w.reshape(k, groups * total)


def _layer_params(l, w_in, a_q_norm_g, a_k_norm_g, c_q_norm_g, c_kv_norm_g, c_w_uq, c_w_ukv,
                  w_branch_a, w_branch_b, w_branch_c, w_o):
    w = w_in[l]
    splits = np.cumsum([0, A_HEADS * HEAD_DIM, A_KV_HEADS * HEAD_DIM, A_KV_HEADS * HEAD_DIM,
                        B_HEADS * HEAD_DIM, B_KV_HEADS * HEAD_DIM, B_KV_HEADS * HEAD_DIM,
                        C_Q_LORA, C_KV_LORA, C_ROPE, D_MODEL, D_MODEL, D_MODEL])
    part = [w[:, splits[i]:splits[i + 1]] for i in range(12)]
    qa, ka, va, qb, kb, vb, cq, ckv, kr, ga, gb, gc = part
    va_e = _pad_cols(va, A_KV_HEADS, HEAD_DIM, LANES)
    kr_p = jnp.pad(kr, ((0, 0), (C_NOPE, LANES - C_NOPE - C_ROPE)))
    vb_e = _pad_cols(vb, B_KV_HEADS, HEAD_DIM, LANES)
    w1 = jnp.concatenate([qa, ka, va_e, qb * (HEAD_DIM ** -0.5 * LOG2E), kb, vb_e, cq, ckv, kr_p],
                         axis=1).astype(BF16)
    gate_a = jnp.concatenate([jnp.tile(a_q_norm_g[l], A_HEADS) * (HEAD_DIM ** -0.5 * LOG2E),
                              jnp.tile(a_k_norm_g[l], A_KV_HEADS)])[None, :].astype(F32)
    wuq = _pad_cols(c_w_uq[l], C_HEADS, C_NOPE + C_ROPE, LANES).astype(BF16)
    ukv = c_w_ukv[l].reshape(C_KV_LORA, C_HEADS, C_NOPE + C_V)
    wuk = jnp.pad(ukv[:, :, :C_NOPE], ((0, 0), (0, 0), (0, LANES - C_NOPE))).reshape(C_KV_LORA, C_HEADS * LANES)
    wuv = jnp.pad(ukv[:, :, C_NOPE:], ((0, 0), (0, 0), (0, LANES - C_V))).reshape(C_KV_LORA, C_HEADS * LANES)
    wukv = jnp.concatenate([wuk, wuv], axis=1).astype(BF16)
    wg = jnp.concatenate([ga, gb, gc], axis=1).astype(BF16)
    return dict(w1=w1, gate_a=gate_a, gcq=c_q_norm_g[l][None, :].astype(F32),
                gckv=c_kv_norm_g[l][None, :].astype(F32), wuq=wuq, wukv=wukv, wg=wg,
                wa=w_branch_a[l].astype(BF16), wb=w_branch_b[l].astype(BF16),
                wc=w_branch_c[l].astype(BF16), wo=w_o[l].astype(BF16))


def kernel(x, ln_in_g, ln_in_b, w_in, a_q_norm_g, a_k_norm_g, b_sink, rpb_table, c_q_norm_g, c_kv_norm_g,
           c_w_uq, c_w_ukv, w_branch_a, w_branch_b, w_branch_c, w_o, ln1_g, ln1_b, router_w, router_bias,
           w_gate, w_up, w_down, ln2_g, ln2_b):
    batch, seq, d = x.shape
    depth = w_in.shape[0]
    n = batch * seq
    alpha = (2 * depth) ** 0.25
    assert d == D_MODEL and seq % PROJ_ROWS == 0 and seq % ATTN_Q_ROWS == 0
    assert seq % (WIN_ROWS * WIN_BLOCKS_PER_STEP) == 0
    assert n % DISPATCH_ROWS == 0 and n % MERGE_ROWS == 0 and n % COMBINE_ROWS == 0

    cos_a, sin_a, cos_q, sin_q, cos_k, sin_k = _rope_tables(seq)
    seg = np.arange(PA_W) // HEAD_DIM
    bd = jnp.asarray((seg[:, None] == seg[None, :]).astype(np.float32) / HEAD_DIM, BF16)
    ones_pat = np.zeros((1, LANES), np.float32)
    ones_pat[0, HEAD_DIM:] = 1.0
    vones_a = jnp.asarray(np.tile(ones_pat, (1, A_KV_HEADS)))
    vones_c = jnp.asarray(np.tile(ones_pat, (1, C_HEADS)))
    tri = jnp.asarray(np.triu(np.ones((MERGE_ROWS, MERGE_ROWS), np.float32), k=1), BF16)
    rwt = router_w.astype(F32).T
    rb = router_bias.astype(F32)[:, None]
    lng = ln_in_g.astype(F32)[None, :]
    lnb = ln_in_b.astype(F32)[None, :]

    tme = EXPERT_ROWS
    n_tiles = (TOP_K * n) // tme + N_EXPERTS
    p_rows = n_tiles * tme

    h = x.reshape(n, d).astype(F32)
    for l in range(depth):
        prm = _layer_params(l, w_in, a_q_norm_g, a_k_norm_g, c_q_norm_g, c_kv_norm_g, c_w_uq, c_w_ukv,
                            w_branch_a, w_branch_b, w_branch_c, w_o)
        outs = _proj_call(h, lng, lnb, prm["w1"], bd, prm["gate_a"], cos_a, sin_a, prm["gcq"], prm["gckv"],
                          prm["wuq"], prm["wukv"], cos_q, sin_q, cos_k, sin_k, vones_a, vones_c,
                          seq=seq, apply_ln=(l == 0))
        if l == 0:
            h, *outs = outs
        qa, ka, va, qb, kb, vb, qc, kc, vc = outs
        oa = _dense_attn_call(qa, ka, va, batch=batch, seq=seq, heads=A_HEADS, kv_heads=A_KV_HEADS,
                              dq=HEAD_DIM, name="attn_a")
        bias = _window_bias(rpb_table, b_sink[l])
        ob = _win_attn_call(qb, kb, vb, bias, batch=batch, seq=seq, heads=B_HEADS, kv_heads=B_KV_HEADS)
        oc = _dense_attn_call(qc, kc, vc, batch=batch, seq=seq, heads=C_HEADS, kv_heads=C_HEADS,
                              dq=LANES, name="attn_c")
        x1, e_idx, wts, rank, cnt = _merge_call(
            h, oa, ob, oc, prm["wg"], prm["wa"], prm["wb"], prm["wc"], prm["wo"],
            ln1_g[l].astype(F32)[None, :], ln1_b[l].astype(F32)[None, :], rwt, rb, tri, alpha=alpha)

        counts = cnt[:, 0].astype(jnp.int32)
        padded = ((counts + tme - 1) // tme) * tme
        ends = jnp.cumsum(padded)
        offs = ends - padded
        pos = rank
        for e in range(N_EXPERTS):
            pos = pos + jnp.where(e_idx == e, offs[e], 0)
        tile_start = jnp.arange(n_tiles, dtype=jnp.int32) * tme
        tile_expert = jnp.minimum(jnp.sum(tile_start[:, None] >= ends[None, :], axis=1), N_EXPERTS - 1)
        tile_expert = tile_expert.astype(jnp.int32)
        n_used = (ends[-1:] // tme).astype(jnp.int32)

        first_row = pos * ROW_TILES

        def per_step(rows):
            return first_row.reshape(TOP_K, n // rows, rows).transpose(1, 0, 2).reshape(n // rows, 1, TOP_K * rows)

        xs = _dispatch_call(per_step(DISPATCH_ROWS), x1, jnp.zeros((p_rows * ROW_TILES, LANES), F32))
        ys = _expert_call(tile_expert, n_used, xs, w_gate.astype(F32), w_up.astype(F32), w_down.astype(F32),
                          layer=l)
        pos_c = per_step(COMBINE_ROWS)
        h = _combine_call(pos_c, ys, x1, wts.T, ln2_g[l].astype(F32)[None, :], ln2_b[l].astype(F32)[None, :],
                          alpha=alpha)
    return h.reshape(batch, seq, d).astype(x.dtype)
```

```python
import functools
import math

import numpy as np
import jax
import jax.numpy as jnp
from jax import lax
from jax.experimental import pallas as pl
from jax.experimental.pallas import tpu as pltpu

F32 = jnp.float32
BF16 = jnp.bfloat16

D_MODEL = 1024
GRID_W = 64
HEAD_DIM = 64
A_HEADS, A_KV_HEADS = 8, 2
B_HEADS, B_KV_HEADS = 8, 2
WINDOW = 128
C_HEADS = 8
C_NOPE, C_ROPE, C_V = 64, 32, 64
C_Q_LORA, C_KV_LORA = 384, 256
N_BUCKETS, MAX_DISTANCE = 32, 128
N_EXPERTS, N_GROUPS, TOP_K = 16, 4, 2
EXPERTS_PER_GROUP = N_EXPERTS // N_GROUPS
D_FF = 512
ROPE_THETA = 10000.0
NORM_EPS = 1e-6
NEG_INF = -1e30

LANES = 128
ROW_TILES = D_MODEL // LANES
BF16_SUBLANES = 16
VT_ROWS = HEAD_DIM + BF16_SUBLANES
LOG2E = math.log2(math.e)
VMEM_LIMIT_BYTES = 48 * 1024 * 1024

PROJ_ROWS = 512
ATTN_Q_ROWS = 512
ATTN_KEY_CHUNK = 256
ATTN_HEADS_INTERLEAVED = 1
WIN_ROWS = 128
WIN_BLOCKS_PER_STEP = 8
MERGE_ROWS = 512
MERGE_COL_TILE = 256
DISPATCH_ROWS = 512
EXPERT_ROWS = 512
EXPERT_COL_TILE = 256
COMBINE_ROWS = 256
DMA_UNROLL = 8

PA_W = A_HEADS * HEAD_DIM + A_KV_HEADS * HEAD_DIM
PA_V = A_KV_HEADS * LANES
PB_W = B_HEADS * HEAD_DIM + B_KV_HEADS * HEAD_DIM + B_KV_HEADS * LANES
PC_W = C_Q_LORA + C_KV_LORA + LANES
OFF_A, OFF_AV = 0, PA_W
OFF_B = OFF_AV + PA_V
OFF_C = OFF_B + PB_W
P1_COLS = OFF_C + PC_W


def _cparams(semantics):
    return pltpu.CompilerParams(dimension_semantics=semantics, vmem_limit_bytes=VMEM_LIMIT_BYTES)


def _layer_norm(x, g, b):
    mu = jnp.mean(x, axis=-1, keepdims=True)
    xc = x - mu
    var = jnp.mean(xc * xc, axis=-1, keepdims=True)
    return xc * lax.rsqrt(var + NORM_EPS) * g + b


def _swap16(x):
    lane = lax.broadcasted_iota(jnp.int32, x.shape, x.ndim - 1)
    up = pltpu.roll(x, LANES - 16, axis=x.ndim - 1)
    down = pltpu.roll(x, 16, axis=x.ndim - 1)
    return jnp.where((lane & 16) == 0, up, down)


def _rope_tiles(x, cos, sin):
    tiles = []
    for c in range(x.shape[-1] // LANES):
        xc = x[:, c * LANES:(c + 1) * LANES]
        tiles.append(xc * cos + _swap16(xc) * sin)
    return jnp.concatenate(tiles, axis=-1) if len(tiles) > 1 else tiles[0]


def _store_vt(vt_ref, v_ext, kv_heads, keep=VT_ROWS):
    vt = v_ext.T
    for kv in range(kv_heads):
        vt_ref[kv * keep:(kv + 1) * keep, :] = vt[kv * LANES:kv * LANES + keep].astype(BF16)


def _proj_kernel(x_ref, lng_ref, lnb_ref, w1_ref, bd_ref, ga_ref, cosa_ref, sina_ref,
                 gcq_ref, gckv_ref, wuq_ref, wukv_ref, cosq_ref, sinq_ref, cosk_ref, sink_ref,
                 vones_a_ref, vones_c_ref,
                 *out_refs, apply_ln):
    if apply_ln:
        xn_ref, qa_ref, ka_ref, va_ref, qb_ref, kb_ref, vb_ref, qc_ref, kc_ref, vc_ref = out_refs
    else:
        qa_ref, ka_ref, va_ref, qb_ref, kb_ref, vb_ref, qc_ref, kc_ref, vc_ref = out_refs
    x = x_ref[...]
    if apply_ln:
        x = _layer_norm(x, lng_ref[...], lnb_ref[...])
        xn_ref[...] = x
    xb = x.astype(BF16)
    pc = jnp.dot(xb, w1_ref[:, OFF_C:OFF_C + PC_W], preferred_element_type=F32)
    pa = jnp.dot(xb, w1_ref[:, OFF_A:OFF_B], preferred_element_type=F32)

    cq = pc[:, :C_Q_LORA]
    ckv = pc[:, C_Q_LORA:C_Q_LORA + C_KV_LORA]
    kr = pc[:, C_Q_LORA + C_KV_LORA:]
    cqn = cq * lax.rsqrt(jnp.mean(cq * cq, axis=-1, keepdims=True) + NORM_EPS) * gcq_ref[...]
    ckvn = ckv * lax.rsqrt(jnp.mean(ckv * ckv, axis=-1, keepdims=True) + NORM_EPS) * gckv_ref[...]
    qc = jnp.dot(cqn.astype(BF16), wuq_ref[...], preferred_element_type=F32)
    kv = jnp.dot(ckvn.astype(BF16), wukv_ref[...], preferred_element_type=F32)
    qk = pa[:, :PA_W]
    ms = jnp.dot((qk * qk).astype(BF16), bd_ref[...], preferred_element_type=F32)
    pb = jnp.dot(xb, w1_ref[:, OFF_B:OFF_C], preferred_element_type=F32)

    qc_ref[...] = _rope_tiles(qc, cosq_ref[...], sinq_ref[...]).astype(BF16)
    kpe = _rope_tiles(kr, cosk_ref[...], sink_ref[...])
    hw = C_HEADS * LANES
    kc_ref[...] = (kv[:, :hw] + jnp.concatenate([kpe] * C_HEADS, axis=-1)).astype(BF16)
    _store_vt(vc_ref, kv[:, hw:] + vones_c_ref[...], C_HEADS)

    qk = qk * lax.rsqrt(ms + NORM_EPS) * ga_ref[...]
    qk = _rope_tiles(qk, cosa_ref[...], sina_ref[...])
    nq = A_HEADS * HEAD_DIM
    qa_ref[...] = qk[:, :nq].astype(BF16)
    kk = qk[:, nq:]
    low = lax.broadcasted_iota(jnp.int32, kk.shape, 1) < HEAD_DIM
    ka_ref[...] = jnp.concatenate([jnp.where(low, kk, 0.0),
                                   jnp.where(low, pltpu.roll(kk, HEAD_DIM, axis=1), 0.0)],
                                  axis=-1).astype(BF16)
    _store_vt(va_ref, pa[:, PA_W:] + vones_a_ref[...], A_KV_HEADS)

    nqb = B_HEADS * HEAD_DIM
    nkb = B_KV_HEADS * HEAD_DIM
    qb_ref[...] = pb[:, :nqb].astype(BF16)
    kb_ref[...] = pb[:, nqb:nqb + nkb].astype(BF16)
    _store_vt(vb_ref, pb[:, nqb + nkb:], B_KV_HEADS, keep=HEAD_DIM)


def _proj_call(x, lng, lnb, w1, bd, ga, cosa, sina, gcq, gckv, wuq, wukv, cosq, sinq, cosk, sink,
               vones_a, vones_c, *, seq, apply_ln):
    n = x.shape[0]
    tm = PROJ_ROWS
    nsb = seq // tm
    row = lambda w: pl.BlockSpec((tm, w), lambda i: (i, 0))
    full = lambda a: pl.BlockSpec(a.shape, lambda i: (0,) * a.ndim, pipeline_mode=pl.Buffered(1))
    tab = pl.BlockSpec((tm, LANES), lambda i: (i % nsb, 0))
    out_w = [A_HEADS * HEAD_DIM, A_KV_HEADS * LANES, -A_KV_HEADS * VT_ROWS,
             B_HEADS * HEAD_DIM, B_KV_HEADS * HEAD_DIM, -B_KV_HEADS * HEAD_DIM,
             C_HEADS * LANES, C_HEADS * LANES, -C_HEADS * VT_ROWS]
    out_shape = [jax.ShapeDtypeStruct((n, w) if w > 0 else (-w, n), BF16) for w in out_w]
    out_specs = [row(w) if w > 0 else pl.BlockSpec((-w, tm), lambda i: (0, i)) for w in out_w]
    if apply_ln:
        out_shape = [jax.ShapeDtypeStruct((n, D_MODEL), F32)] + out_shape
        out_specs = [row(D_MODEL)] + out_specs
    return pl.pallas_call(
        functools.partial(_proj_kernel, apply_ln=apply_ln),
        out_shape=out_shape,
        grid=(n // tm,),
        in_specs=[row(D_MODEL), full(lng), full(lnb), full(w1), full(bd), full(ga), tab, tab,
                  full(gcq), full(gckv), full(wuq), full(wukv), tab, tab, tab, tab,
                  full(vones_a), full(vones_c)],
        out_specs=out_specs,
        compiler_params=_cparams(("parallel",)),
        name="proj_ln" if apply_ln else "proj",
    )(x, lng, lnb, w1, bd, ga, cosa, sina, gcq, gckv, wuq, wukv, cosq, sinq, cosk, sink,
      vones_a, vones_c)


def _dense_attn_kernel(q_ref, k_ref, vt_ref, o_ref, *, heads, kv_heads, dq):
    group = heads // kv_heads
    seq = k_ref.shape[0]
    tq = q_ref.shape[0]
    n_chunks = seq // ATTN_KEY_CHUNK
    units = [(h0 + dh, c) for h0 in range(0, heads, ATTN_HEADS_INTERLEAVED) for c in range(n_chunks)
             for dh in range(ATTN_HEADS_INTERLEAVED)]

    def scores_t(h, c):
        kv = h // group
        qh = q_ref[:, h * dq:(h + 1) * dq]
        kc = k_ref[c * ATTN_KEY_CHUNK:(c + 1) * ATTN_KEY_CHUNK, kv * LANES:(kv + 1) * LANES]
        if dq < LANES:
            kc = kc[:, :dq]
        return lax.dot_general(kc, qh, (((1,), (1,)), ((), ())), preferred_element_type=F32)

    ahead = 2
    st = {u: scores_t(*u) for u in units[:ahead]}
    outs = [None] * heads
    m, acc = {}, {}
    for idx, (h, c) in enumerate(units):
        kv = h // group
        s = st.pop((h, c))
        if idx + ahead < len(units):
            st[units[idx + ahead]] = scores_t(*units[idx + ahead])
        cmax = jnp.max(s, axis=0, keepdims=True)
        m_new = cmax if c == 0 else jnp.maximum(m[h], cmax)
        pt = jnp.exp2(s - m_new).astype(BF16)
        vtc = vt_ref[kv * VT_ROWS:(kv + 1) * VT_ROWS, c * ATTN_KEY_CHUNK:(c + 1) * ATTN_KEY_CHUNK]
        pv = jnp.dot(vtc, pt, preferred_element_type=F32)
        acc[h] = pv if c == 0 else acc[h] * jnp.exp2(m[h] - m_new) + pv
        m[h] = m_new
        if c == n_chunks - 1:
            outs[h] = acc[h][:HEAD_DIM] / acc[h][HEAD_DIM:HEAD_DIM + 1]
    o_ref[...] = jnp.concatenate(outs, axis=0).T.astype(o_ref.dtype)


def _dense_attn_call(q, k, vt, *, batch, seq, heads, kv_heads, dq, name):
    n = q.shape[0]
    tq = ATTN_Q_ROWS
    nq = seq // tq
    return pl.pallas_call(
        functools.partial(_dense_attn_kernel, heads=heads, kv_heads=kv_heads, dq=dq),
        out_shape=jax.ShapeDtypeStruct((n, heads * HEAD_DIM), BF16),
        grid=(batch, nq),
        in_specs=[pl.BlockSpec((tq, heads * dq), lambda b, i: (b * nq + i, 0)),
                  pl.BlockSpec((seq, kv_heads * LANES), lambda b, i: (b, 0)),
                  pl.BlockSpec((kv_heads * VT_ROWS, seq), lambda b, i: (0, b))],
        out_specs=pl.BlockSpec((tq, heads * HEAD_DIM), lambda b, i: (b * nq + i, 0)),
        compiler_params=_cparams(("parallel", "parallel")),
        name=name,
    )(q, k, vt)


def _win_attn_kernel(q_ref, k_ref, vt_ref, bias_ref, o_ref, *, heads, kv_heads, nb):
    group = heads // kv_heads
    tq = WIN_ROWS
    span = 3 * tq
    for j in range(WIN_BLOCKS_PER_STEP):
        i = pl.program_id(1) * WIN_BLOCKS_PER_STEP + j
        outs = []
        for kv in range(kv_heads):
            qs = jnp.concatenate(
                [q_ref[j * tq:(j + 1) * tq, (kv * group + g) * HEAD_DIM:(kv * group + g + 1) * HEAD_DIM]
                 for g in range(group)], axis=0)
            bias = bias_ref[kv]
            logit_chunks, vt_chunks = [], []
            for c in range(3):
                blk = i + (c - 1)
                start = pl.multiple_of(jnp.clip(blk, 0, nb - 1) * tq, tq)
                kc = k_ref[pl.ds(start, tq), kv * HEAD_DIM:(kv + 1) * HEAD_DIM]
                vt_chunks.append(vt_ref[kv * HEAD_DIM:(kv + 1) * HEAD_DIM, pl.ds(start, tq)])
                sc = lax.dot_general(kc, qs, (((1,), (1,)), ((), ())), preferred_element_type=F32)
                sc = sc + bias[c * tq:(c + 1) * tq]
                if c != 1:
                    sc = jnp.where(jnp.logical_and(blk >= 0, blk <= nb - 1), sc, NEG_INF)
                logit_chunks.append(sc)
            logit_chunks.append(bias[span:])
            logits = jnp.concatenate(logit_chunks, axis=0)
            m = jnp.max(logits, axis=0, keepdims=True)
            pr = jnp.exp2(logits - m)
            denom = jnp.sum(pr, axis=0, keepdims=True)
            vt = jnp.concatenate(vt_chunks, axis=1)
            ot = jnp.dot(vt, pr[:span].astype(BF16), preferred_element_type=F32) / denom
            outs += [ot[:, g * tq:(g + 1) * tq] for g in range(group)]
        o_ref[j * tq:(j + 1) * tq, :] = jnp.concatenate(outs, axis=0).T.astype(o_ref.dtype)


def _win_attn_call(q, k, vt, bias, *, batch, seq, heads, kv_heads):
    n = q.shape[0]
    ts = WIN_ROWS * WIN_BLOCKS_PER_STEP
    nq = seq // ts
    return pl.pallas_call(
        functools.partial(_win_attn_kernel, heads=heads, kv_heads=kv_heads, nb=seq // WIN_ROWS),
        out_shape=jax.ShapeDtypeStruct((n, heads * HEAD_DIM), BF16),
        grid=(batch, nq),
        in_specs=[pl.BlockSpec((ts, heads * HEAD_DIM), lambda b, i: (b * nq + i, 0)),
                  pl.BlockSpec((seq, kv_heads * HEAD_DIM), lambda b, i: (b, 0)),
                  pl.BlockSpec((kv_heads * HEAD_DIM, seq), lambda b, i: (0, b)),
                  pl.BlockSpec(bias.shape, lambda b, i: (0, 0, 0))],
        out_specs=pl.BlockSpec((ts, heads * HEAD_DIM), lambda b, i: (b * nq + i, 0)),
        compiler_params=_cparams(("parallel", "parallel")),
        name="win_attn",
    )(q, k, vt, bias)


def _store_row_tiles(ref, x):
    rows = x.shape[0]
    for c in range(ROW_TILES):
        ref[pl.ds(c, rows, stride=ROW_TILES), :] = x[:, c * LANES:(c + 1) * LANES]


def _load_row_tiles(ref):
    rows = ref.shape[0] // ROW_TILES
    return jnp.concatenate([ref[pl.ds(c, rows, stride=ROW_TILES), :] for c in range(ROW_TILES)], axis=-1)


def _merge_kernel(x_ref, oa_ref, ob_ref, oc_ref, wg_ref, wa_ref, wb_ref, wc_ref, wo_ref,
                  g_ref, b_ref, rwt_ref, rb_ref, tri_ref,
                  x1_ref, e_ref, w_ref, r_ref, cnt_ref, carry_ref, *, alpha):
    step = pl.program_id(0)

    @pl.when(step == 0)
    def _():
        carry_ref[...] = jnp.zeros_like(carry_ref)

    x = x_ref[...]
    xb = x.astype(BF16)
    d = D_MODEL
    mixers = ((oa_ref[...], wa_ref), (ob_ref[...], wb_ref), (oc_ref[...], wc_ref))
    tiles = []
    for c0 in range(0, d, MERGE_COL_TILE):
        acc = None
        for idx, (o, wbr_ref) in enumerate(mixers):
            gate = jax.nn.sigmoid(jnp.dot(xb, wg_ref[:, idx * d + c0:idx * d + c0 + MERGE_COL_TILE],
                                          preferred_element_type=F32))
            br = jnp.dot(o, wbr_ref[:, c0:c0 + MERGE_COL_TILE], preferred_element_type=F32)
            acc = gate * br if acc is None else acc + gate * br
        tiles.append(acc.astype(BF16))
    merged = jnp.concatenate(tiles, axis=-1)
    mix = jnp.dot(merged, wo_ref[...], preferred_element_type=F32)
    x1 = _layer_norm(alpha * x + mix, g_ref[...], b_ref[...])
    _store_row_tiles(x1_ref, x1)

    logits = lax.dot_general(rwt_ref[...], x1, (((1,), (1,)), ((), ())),
                             preferred_element_type=F32, precision=lax.Precision.HIGHEST)
    scores = jax.nn.sigmoid(logits)
    biased = scores + rb_ref[...]
    shape = biased.shape
    rows = lax.broadcasted_iota(jnp.int32, shape, 0)
    sub = lax.broadcasted_iota(jnp.int32, (EXPERTS_PER_GROUP, shape[1]), 0)
    best, sel = None, None
    for g in range(N_GROUPS):
        blk = biased[g * EXPERTS_PER_GROUP:(g + 1) * EXPERTS_PER_GROUP]
        m1 = jnp.max(blk, axis=0, keepdims=True)
        i1 = jnp.min(jnp.where(blk == m1, sub, EXPERTS_PER_GROUP), axis=0, keepdims=True)
        m2 = jnp.max(jnp.where(sub == i1, -jnp.inf, blk), axis=0, keepdims=True)
        gs = m1 + m2
        if best is None:
            best, sel = gs, jnp.zeros_like(i1)
        else:
            better = gs > best
            sel = jnp.where(better, g, sel)
            best = jnp.where(better, gs, best)
    in_group = (rows >> 2) == sel
    bm = jnp.where(in_group, biased, -jnp.inf)
    t1 = jnp.max(bm, axis=0, keepdims=True)
    e0 = jnp.min(jnp.where(bm == t1, rows, N_EXPERTS), axis=0, keepdims=True)
    bm2 = jnp.where(rows == e0, -jnp.inf, bm)
    t2 = jnp.max(bm2, axis=0, keepdims=True)
    e1 = jnp.min(jnp.where(bm2 == t2, rows, N_EXPERTS), axis=0, keepdims=True)
    hot0 = rows == e0
    hot1 = rows == e1
    s0 = jnp.sum(jnp.where(hot0, scores, 0.0), axis=0, keepdims=True)
    s1 = jnp.sum(jnp.where(hot1, scores, 0.0), axis=0, keepdims=True)
    tot = s0 + s1
    e_ref[...] = jnp.concatenate([e0, e1], axis=0)
    w_ref[...] = jnp.concatenate([s0 / tot, s1 / tot], axis=0)

    hot = jnp.where(jnp.logical_or(hot0, hot1), 1.0, 0.0)
    prefix = jnp.dot(hot.astype(BF16), tri_ref[...], preferred_element_type=F32) + carry_ref[:, 0:1]
    r0 = jnp.sum(jnp.where(hot0, prefix, 0.0), axis=0, keepdims=True)
    r1 = jnp.sum(jnp.where(hot1, prefix, 0.0), axis=0, keepdims=True)
    r_ref[...] = jnp.concatenate([r0, r1], axis=0).astype(jnp.int32)
    carry_ref[...] = carry_ref[...] + jnp.sum(hot, axis=1, keepdims=True)
    cnt_ref[...] = carry_ref[...]


def _merge_call(x, oa, ob, oc, wg, wa, wb, wc, wo, g, b, rwt, rb, tri, *, alpha):
    n = x.shape[0]
    tm = MERGE_ROWS
    row = lambda w: pl.BlockSpec((tm, w), lambda i: (i, 0))
    full = lambda a: pl.BlockSpec(a.shape, lambda i: (0,) * a.ndim, pipeline_mode=pl.Buffered(1))
    tok = pl.BlockSpec((TOP_K, tm), lambda i: (0, i))
    slabs = pl.BlockSpec((tm * ROW_TILES, LANES), lambda i: (i, 0))
    return pl.pallas_call(
        functools.partial(_merge_kernel, alpha=alpha),
        out_shape=[jax.ShapeDtypeStruct((n * ROW_TILES, LANES), F32),
                   jax.ShapeDtypeStruct((TOP_K, n), jnp.int32),
                   jax.ShapeDtypeStruct((TOP_K, n), F32),
                   jax.ShapeDtypeStruct((TOP_K, n), jnp.int32),
                   jax.ShapeDtypeStruct((N_EXPERTS, LANES), F32)],
        grid=(n // tm,),
        in_specs=[row(D_MODEL), row(oa.shape[1]), row(ob.shape[1]), row(oc.shape[1]),
                  full(wg), full(wa), full(wb), full(wc), full(wo), full(g), full(b),
                  full(rwt), full(rb), full(tri)],
        out_specs=[slabs, tok, tok, tok,
                   pl.BlockSpec((N_EXPERTS, LANES), lambda i: (0, 0))],
        scratch_shapes=[pltpu.VMEM((N_EXPERTS, LANES), F32)],
        compiler_params=_cparams(("arbitrary",)),
        name="merge",
    )(x, oa, ob, oc, wg, wa, wb, wc, wo, g, b, rwt, rb, tri)


def _row_slab(ref, first):
    if not isinstance(first, int):
        first = pl.multiple_of(first, ROW_TILES)
    return ref.at[pl.ds(first, ROW_TILES)]


def _dispatch_kernel(pad_first_ref, pad_rows_ref, tail_ref, pos_ref, x_ref, dst_hbm, zero_ref, sem, zero_sem):
    tb = DISPATCH_ROWS

    @pl.when(pl.program_id(0) == 0)
    def _():
        zero_ref[...] = jnp.zeros_like(zero_ref)

        def for_each_fill(act):
            for e in range(N_EXPERTS):
                first, n = pad_first_ref[e], pad_rows_ref[e]
                for bit in reversed(range(EXPERT_ROWS.bit_length() - 1)):
                    run = (1 << bit) * ROW_TILES
                    has = (n & (1 << bit)) != 0

                    @pl.when(has)
                    def _(first=first, run=run):
                        act(pltpu.make_async_copy(zero_ref.at[pl.ds(0, run)],
                                                  dst_hbm.at[pl.ds(pl.multiple_of(first, ROW_TILES), run)],
                                                  zero_sem))
                    first = first + jnp.where(has, run, 0)
            tile = EXPERT_ROWS * ROW_TILES
            for k in range(N_EXPERTS):
                @pl.when(k < tail_ref[1])
                def _(k=k):
                    act(pltpu.make_async_copy(
                        zero_ref, dst_hbm.at[pl.ds(pl.multiple_of((tail_ref[0] + k) * tile, tile), tile)], zero_sem))

        for_each_fill(lambda copy: copy.start())
        for_each_fill(lambda copy: copy.wait())

    def start(g, carry):
        for u in range(DMA_UNROLL):
            t = g * DMA_UNROLL + u
            for j in range(TOP_K):
                pltpu.make_async_copy(_row_slab(x_ref, t * ROW_TILES),
                                      _row_slab(dst_hbm, pos_ref[0, 0, j * tb + t]),
                                      sem).start(priority=(u + j) % 2)
        return carry

    lax.fori_loop(0, tb // DMA_UNROLL, start, 0)
    for j in range(TOP_K):
        pltpu.make_async_copy(x_ref, dst_hbm.at[pl.ds(0, tb * ROW_TILES)], sem).wait()


def _dispatch_call(pad_first, pad_rows, tail, pos_flat, x1, *, p_rows):
    nsteps = pos_flat.shape[0]
    tb = DISPATCH_ROWS
    grid_spec = pltpu.PrefetchScalarGridSpec(
        num_scalar_prefetch=3,
        grid=(nsteps,),
        in_specs=[pl.BlockSpec((1, 1, TOP_K * tb), lambda i, *_: (i, 0, 0), memory_space=pltpu.SMEM),
                  pl.BlockSpec((tb * ROW_TILES, LANES), lambda i, *_: (i, 0))],
        out_specs=pl.BlockSpec(memory_space=pl.ANY),
        scratch_shapes=[pltpu.VMEM((EXPERT_ROWS * ROW_TILES, LANES), F32),
                        pltpu.SemaphoreType.DMA(()), pltpu.SemaphoreType.DMA(())],
    )
    return pl.pallas_call(
        _dispatch_kernel,
        out_shape=jax.ShapeDtypeStruct((p_rows * ROW_TILES, LANES), F32),
        grid_spec=grid_spec,
        compiler_params=pltpu.CompilerParams(dimension_semantics=("arbitrary",), has_side_effects=True,
                                             vmem_limit_bytes=VMEM_LIMIT_BYTES),
        name="dispatch",
    )(pad_first, pad_rows, tail, pos_flat, x1)


def _expert_kernel(te_ref, nu_ref, xs_ref, wg_ref, wu_ref, wd_ref, ys_ref, wgu_bf, wd_bf):
    i = pl.program_id(0)

    @pl.when(jnp.logical_or(i == 0, te_ref[i] != te_ref[jnp.maximum(i - 1, 0)]))
    def _():
        wgu_bf[:, :D_FF] = wg_ref[0, 0].astype(BF16)
        wgu_bf[:, D_FF:] = wu_ref[0, 0].astype(BF16)
        wd_bf[...] = wd_ref[0, 0].astype(BF16)

    @pl.when(i < nu_ref[0])
    def _():
        xb = _load_row_tiles(xs_ref).astype(BF16)
        hid = []
        for c0 in range(0, D_FF, EXPERT_COL_TILE):
            gate = jnp.dot(xb, wgu_bf[:, c0:c0 + EXPERT_COL_TILE], preferred_element_type=F32)
            up = jnp.dot(xb, wgu_bf[:, D_FF + c0:D_FF + c0 + EXPERT_COL_TILE], preferred_element_type=F32)
            hid.append(((gate * jax.nn.sigmoid(gate)) * up).astype(BF16))
        y = jnp.dot(jnp.concatenate(hid, axis=-1), wd_bf[...], preferred_element_type=F32)
        _store_row_tiles(ys_ref, y)

    @pl.when(i >= nu_ref[0])
    def _():
        ys_ref[...] = jnp.zeros_like(ys_ref)


def _expert_call(tile_expert, n_used, xs, w_gate, w_up, w_down, *, layer):
    p = xs.shape[0] // ROW_TILES
    tm = EXPERT_ROWS
    wspec = lambda r, c: pl.BlockSpec((1, 1, r, c), lambda i, te, nu: (layer, te[i], 0, 0))
    grid_spec = pltpu.PrefetchScalarGridSpec(
        num_scalar_prefetch=2,
        grid=(p // tm,),
        in_specs=[pl.BlockSpec((tm * ROW_TILES, LANES), lambda i, te, nu: (i, 0)),
                  wspec(D_MODEL, D_FF), wspec(D_MODEL, D_FF), wspec(D_FF, D_MODEL)],
        out_specs=pl.BlockSpec((tm * ROW_TILES, LANES), lambda i, te, nu: (i, 0)),
        scratch_shapes=[pltpu.VMEM((D_MODEL, 2 * D_FF), BF16), pltpu.VMEM((D_FF, D_MODEL), BF16)],
    )
    return pl.pallas_call(
        _expert_kernel,
        out_shape=jax.ShapeDtypeStruct((p * ROW_TILES, LANES), F32),
        grid_spec=grid_spec,
        compiler_params=_cparams(("arbitrary",)),
        name="experts",
    )(tile_expert, n_used, xs, w_gate, w_up, w_down)


def _combine_kernel(pos_ref, nxt_ref, ys_hbm, x_ref, w_ref, g_ref, b_ref, o_ref, buf_ref, sem, *, alpha):
    tc = COMBINE_ROWS
    i = pl.program_id(0)
    slot = i % 2

    def row_copy(idx_ref, to_slot, t, j):
        return pltpu.make_async_copy(_row_slab(ys_hbm, idx_ref[0, 0, j * tc + t]),
                                     _row_slab(buf_ref.at[to_slot, j], t * ROW_TILES), sem.at[to_slot])

    def wait_slot(s):
        for j in range(TOP_K):
            pltpu.make_async_copy(ys_hbm.at[pl.ds(0, tc * ROW_TILES)], buf_ref.at[s, j], sem.at[s]).wait()

    @pl.when(i == 0)
    def _():
        def start(g, carry):
            for u in range(DMA_UNROLL):
                for j in range(TOP_K):
                    row_copy(pos_ref, 0, g * DMA_UNROLL + u, j).start(priority=(u + j) % 2)
            return carry
        lax.fori_loop(0, tc // DMA_UNROLL, start, 0)

    for t in range(tc):
        for j in range(TOP_K):
            row_copy(nxt_ref, 1 - slot, t, j).start(priority=(t + j) % 2)
    wait_slot(slot)
    w = w_ref[...]
    ffn = (w[:, 0:1] * _load_row_tiles(buf_ref.at[slot, 0])
           + w[:, 1:2] * _load_row_tiles(buf_ref.at[slot, 1]))
    o_ref[...] = _layer_norm(alpha * _load_row_tiles(x_ref) + ffn, g_ref[...], b_ref[...])

    @pl.when(i == pl.num_programs(0) - 1)
    def _():
        wait_slot(1 - slot)


def _combine_call(pos_flat, ys, x1, wts_t, g, b, *, alpha):
    n = x1.shape[0] // ROW_TILES
    tc = COMBINE_ROWS
    nsteps = n // tc
    idx = lambda shift: pl.BlockSpec((1, 1, TOP_K * tc), lambda i: (jnp.minimum(i + shift, nsteps - 1), 0, 0),
                                     memory_space=pltpu.SMEM)
    return pl.pallas_call(
        functools.partial(_combine_kernel, alpha=alpha),
        out_shape=jax.ShapeDtypeStruct((n, D_MODEL), F32),
        grid=(nsteps,),
        in_specs=[idx(0), idx(1),
                  pl.BlockSpec(memory_space=pl.ANY),
                  pl.BlockSpec((tc * ROW_TILES, LANES), lambda i: (i, 0)),
                  pl.BlockSpec((tc, TOP_K), lambda i: (i, 0)),
                  pl.BlockSpec(g.shape, lambda i: (0, 0)),
                  pl.BlockSpec(b.shape, lambda i: (0, 0))],
        out_specs=pl.BlockSpec((tc, D_MODEL), lambda i: (i, 0)),
        scratch_shapes=[pltpu.VMEM((2, TOP_K, tc * ROW_TILES, LANES), F32),
                        pltpu.SemaphoreType.DMA((2,))],
        compiler_params=_cparams(("arbitrary",)),
        name="combine",
    )(pos_flat, pos_flat, ys, x1, wts_t, g, b)


def _t5_bucket(rel):
    nb = N_BUCKETS // 2
    ret = (rel > 0).astype(np.int32) * nb
    n = np.abs(rel)
    max_exact = nb // 2
    large = max_exact + (np.log(np.maximum(n, 1) / max_exact)
                         / math.log(MAX_DISTANCE / max_exact) * (nb - max_exact)).astype(np.int32)
    large = np.minimum(large, nb - 1)
    return (ret + np.where(n < max_exact, n, large)).astype(np.int32)


def _rope_tables(seq):
    def cs(pos, dim):
        inv = ROPE_THETA ** (-(jnp.arange(0, dim, 2, dtype=F32) / dim))
        ang = pos.astype(F32)[:, None] * inv[None, :]
        return jnp.cos(ang), jnp.sin(ang)

    t = jnp.arange(seq)
    cr, sr = cs(t // GRID_W, HEAD_DIM // 2)
    cc, sc = cs(t % GRID_W, HEAD_DIM // 2)
    cos_head = jnp.concatenate([cr, cr, cc, cc], axis=-1)
    sin_head = jnp.concatenate([-sr, sr, -sc, sc], axis=-1)
    cos_a = jnp.concatenate([cos_head, cos_head], axis=-1)
    sin_a = jnp.concatenate([sin_head, sin_head], axis=-1)
    cs_, ss_ = cs(t, C_ROPE)
    ones = jnp.ones((seq, C_NOPE), F32)
    pad1 = jnp.ones((seq, LANES - C_NOPE - C_ROPE), F32)
    cos_k = jnp.concatenate([ones, cs_, cs_, pad1], axis=-1)
    sin_k = jnp.concatenate([0 * ones, -ss_, ss_, 0 * pad1], axis=-1)
    scale = (C_NOPE + C_ROPE) ** -0.5 * LOG2E
    return cos_a, sin_a, cos_k * scale, sin_k * scale, cos_k, sin_k


def _window_bias(rpb_table, b_sink_l):
    span = WIN_ROWS + 2 * WINDOW
    rel = np.arange(span)[None, :] - WINDOW - np.arange(WIN_ROWS)[:, None]
    band = jnp.asarray(np.abs(rel) <= WINDOW)
    onehot = jnp.asarray(_t5_bucket(rel)[:, :, None] == np.arange(N_BUCKETS), F32)
    bias = jnp.einsum("qsb,bh->qsh", onehot, rpb_table.astype(F32),
                      precision=lax.Precision.HIGHEST)
    bias = jnp.where(band[:, :, None], bias * LOG2E, NEG_INF)
    bias = jnp.transpose(bias, (1, 2, 0))
    sink = jnp.full((8, B_HEADS, WIN_ROWS), NEG_INF, F32)
    sink = sink.at[0].set(jnp.broadcast_to(b_sink_l.astype(F32)[:, None] * LOG2E, (B_HEADS, WIN_ROWS)))
    ext = jnp.concatenate([bias, sink], axis=0)
    group = B_HEADS // B_KV_HEADS
    ext = ext.reshape(span + 8, B_KV_HEADS, group * WIN_ROWS)
    return jnp.transpose(ext, (1, 0, 2))


def _pad_cols(w, groups, width, total):
    k = w.shape[0]
    w = w.reshape(k, groups, width)
    w = jnp.pad(w, ((0, 0), (0, 0), (0, total - width)))
    return w.reshape(k, groups * total)


def _layer_params(l, w_in, a_q_norm_g, a_k_norm_g, c_q_norm_g, c_kv_norm_g, c_w_uq, c_w_ukv,
                  w_branch_a, w_branch_b, w_branch_c, w_o):
    w = w_in[l]
    splits = np.cumsum([0, A_HEADS * HEAD_DIM, A_KV_HEADS * HEAD_DIM, A_KV_HEADS * HEAD_DIM,
                        B_HEADS * HEAD_DIM, B_KV_HEADS * HEAD_DIM, B_KV_HEADS * HEAD_DIM,
                        C_Q_LORA, C_KV_LORA, C_ROPE, D_MODEL, D_MODEL, D_MODEL])
    part = [w[:, splits[i]:splits[i + 1]] for i in range(12)]
    qa, ka, va, qb, kb, vb, cq, ckv, kr, ga, gb, gc = part
    va_e = _pad_cols(va, A_KV_HEADS, HEAD_DIM, LANES)
    kr_p = jnp.pad(kr, ((0, 0), (C_NOPE, LANES - C_NOPE - C_ROPE)))
    vb_e = _pad_cols(vb, B_KV_HEADS, HEAD_DIM, LANES)
    w1 = jnp.concatenate([qa, ka, va_e, qb * (HEAD_DIM ** -0.5 * LOG2E), kb, vb_e, cq, ckv, kr_p],
                         axis=1).astype(BF16)
    gate_a = jnp.concatenate([jnp.tile(a_q_norm_g[l], A_HEADS) * (HEAD_DIM ** -0.5 * LOG2E),
                              jnp.tile(a_k_norm_g[l], A_KV_HEADS)])[None, :].astype(F32)
    wuq = _pad_cols(c_w_uq[l], C_HEADS, C_NOPE + C_ROPE, LANES).astype(BF16)
    ukv = c_w_ukv[l].reshape(C_KV_LORA, C_HEADS, C_NOPE + C_V)
    wuk = jnp.pad(ukv[:, :, :C_NOPE], ((0, 0), (0, 0), (0, LANES - C_NOPE))).reshape(C_KV_LORA, C_HEADS * LANES)
    wuv = jnp.pad(ukv[:, :, C_NOPE:], ((0, 0), (0, 0), (0, LANES - C_V))).reshape(C_KV_LORA, C_HEADS * LANES)
    wukv = jnp.concatenate([wuk, wuv], axis=1).astype(BF16)
    wg = jnp.concatenate([ga, gb, gc], axis=1).astype(BF16)
    return dict(w1=w1, gate_a=gate_a, gcq=c_q_norm_g[l][None, :].astype(F32),
                gckv=c_kv_norm_g[l][None, :].astype(F32), wuq=wuq, wukv=wukv, wg=wg,
                wa=w_branch_a[l].astype(BF16), wb=w_branch_b[l].astype(BF16),
                wc=w_branch_c[l].astype(BF16), wo=w_o[l].astype(BF16))


def kernel(x, ln_in_g, ln_in_b, w_in, a_q_norm_g, a_k_norm_g, b_sink, rpb_table, c_q_norm_g, c_kv_norm_g,
           c_w_uq, c_w_ukv, w_branch_a, w_branch_b, w_branch_c, w_o, ln1_g, ln1_b, router_w, router_bias,
           w_gate, w_up, w_down, ln2_g, ln2_b):
    batch, seq, d = x.shape
    depth = w_in.shape[0]
    n = batch * seq
    alpha = (2 * depth) ** 0.25
    assert d == D_MODEL and seq % PROJ_ROWS == 0 and seq % ATTN_Q_ROWS == 0
    assert seq % (WIN_ROWS * WIN_BLOCKS_PER_STEP) == 0
    assert n % DISPATCH_ROWS == 0 and n % MERGE_ROWS == 0 and n % COMBINE_ROWS == 0

    cos_a, sin_a, cos_q, sin_q, cos_k, sin_k = _rope_tables(seq)
    seg = np.arange(PA_W) // HEAD_DIM
    bd = jnp.asarray((seg[:, None] == seg[None, :]).astype(np.float32) / HEAD_DIM, BF16)
    ones_pat = np.zeros((1, LANES), np.float32)
    ones_pat[0, HEAD_DIM:] = 1.0
    vones_a = jnp.asarray(np.tile(ones_pat, (1, A_KV_HEADS)))
    vones_c = jnp.asarray(np.tile(ones_pat, (1, C_HEADS)))
    tri = jnp.asarray(np.triu(np.ones((MERGE_ROWS, MERGE_ROWS), np.float32), k=1), BF16)
    rwt = router_w.astype(F32).T
    rb = router_bias.astype(F32)[:, None]
    lng = ln_in_g.astype(F32)[None, :]
    lnb = ln_in_b.astype(F32)[None, :]

    tme = EXPERT_ROWS
    n_tiles = (TOP_K * n) // tme + N_EXPERTS
    p_rows = n_tiles * tme

    h = x.reshape(n, d).astype(F32)
    for l in range(depth):
        prm = _layer_params(l, w_in, a_q_norm_g, a_k_norm_g, c_q_norm_g, c_kv_norm_g, c_w_uq, c_w_ukv,
                            w_branch_a, w_branch_b, w_branch_c, w_o)
        outs = _proj_call(h, lng, lnb, prm["w1"], bd, prm["gate_a"], cos_a, sin_a, prm["gcq"], prm["gckv"],
                          prm["wuq"], prm["wukv"], cos_q, sin_q, cos_k, sin_k, vones_a, vones_c,
                          seq=seq, apply_ln=(l == 0))
        if l == 0:
            h, *outs = outs
        qa, ka, va, qb, kb, vb, qc, kc, vc = outs
        oa = _dense_attn_call(qa, ka, va, batch=batch, seq=seq, heads=A_HEADS, kv_heads=A_KV_HEADS,
                              dq=HEAD_DIM, name="attn_a")
        bias = _window_bias(rpb_table, b_sink[l])
        ob = _win_attn_call(qb, kb, vb, bias, batch=batch, seq=seq, heads=B_HEADS, kv_heads=B_KV_HEADS)
        oc = _dense_attn_call(qc, kc, vc, batch=batch, seq=seq, heads=C_HEADS, kv_heads=C_HEADS,
                              dq=LANES, name="attn_c")
        x1, e_idx, wts, rank, cnt = _merge_call(
            h, oa, ob, oc, prm["wg"], prm["wa"], prm["wb"], prm["wc"], prm["wo"],
            ln1_g[l].astype(F32)[None, :], ln1_b[l].astype(F32)[None, :], rwt, rb, tri, alpha=alpha)

        counts = cnt[:, 0].astype(jnp.int32)
        padded = ((counts + tme - 1) // tme) * tme
        ends = jnp.cumsum(padded)
        offs = ends - padded
        pos = rank
        for e in range(N_EXPERTS):
            pos = pos + jnp.where(e_idx == e, offs[e], 0)
        tile_start = jnp.arange(n_tiles, dtype=jnp.int32) * tme
        tile_expert = jnp.minimum(jnp.sum(tile_start[:, None] >= ends[None, :], axis=1), N_EXPERTS - 1)
        tile_expert = tile_expert.astype(jnp.int32)
        n_used = (ends[-1:] // tme).astype(jnp.int32)

        first_row = pos * ROW_TILES

        def per_step(rows):
            return first_row.reshape(TOP_K, n // rows, rows).transpose(1, 0, 2).reshape(n // rows, 1, TOP_K * rows)

        pad_first = ((offs + counts) * ROW_TILES).astype(jnp.int32)
        pad_rows = (padded - counts).astype(jnp.int32)
        tail = jnp.concatenate([n_used, n_tiles - n_used]).astype(jnp.int32)
        xs = _dispatch_call(pad_first, pad_rows, tail, per_step(DISPATCH_ROWS), x1, p_rows=p_rows)
        ys = _expert_call(tile_expert, n_used, xs, w_gate.astype(F32), w_up.astype(F32), w_down.astype(F32),
                          layer=l)
        pos_c = per_step(COMBINE_ROWS)
        h = _combine_call(pos_c, ys, x1, wts.T, ln2_g[l].astype(F32)[None, :], ln2_b[l].astype(F32)[None, :],
                          alpha=alpha)
    return h.reshape(batch, seq, d).astype(x.dtype)
```

```python
import functools
import math

import numpy as np
import jax
import jax.numpy as jnp
from jax import lax
from jax.experimental import pallas as pl
from jax.experimental.pallas import tpu as pltpu

F32 = jnp.float32
BF16 = jnp.bfloat16

D_MODEL = 1024
GRID_W = 64
HEAD_DIM = 64
A_HEADS, A_KV_HEADS = 8, 2
B_HEADS, B_KV_HEADS = 8, 2
WINDOW = 128
C_HEADS = 8
C_NOPE, C_ROPE, C_V = 64, 32, 64
C_Q_LORA, C_KV_LORA = 384, 256
N_BUCKETS, MAX_DISTANCE = 32, 128
N_EXPERTS, N_GROUPS, TOP_K = 16, 4, 2
EXPERTS_PER_GROUP = N_EXPERTS // N_GROUPS
D_FF = 512
ROPE_THETA = 10000.0
NORM_EPS = 1e-6
NEG_INF = -1e30

LANES = 128
ROW_TILES = D_MODEL // LANES
BF16_SUBLANES = 16
VT_ROWS = HEAD_DIM + BF16_SUBLANES
LOG2E = math.log2(math.e)
VMEM_LIMIT_BYTES = 48 * 1024 * 1024

PROJ_ROWS = 512
ATTN_Q_ROWS = 512
ATTN_KEY_CHUNK = 256
ATTN_HEADS_INTERLEAVED = 1
WIN_ROWS = 128
WIN_BLOCKS_PER_STEP = 16
MERGE_ROWS = 512
MERGE_COL_TILE = 256
DISPATCH_ROWS = 512
EXPERT_ROWS = 512
EXPERT_COL_TILE = 256
COMBINE_ROWS = 256
DMA_UNROLL = 8

PA_W = A_HEADS * HEAD_DIM + A_KV_HEADS * HEAD_DIM
PA_V = A_KV_HEADS * LANES
PB_W = B_HEADS * HEAD_DIM + B_KV_HEADS * HEAD_DIM + B_KV_HEADS * LANES
PC_W = C_Q_LORA + C_KV_LORA + LANES
OFF_A, OFF_AV = 0, PA_W
OFF_B = OFF_AV + PA_V
OFF_C = OFF_B + PB_W
P1_COLS = OFF_C + PC_W


def _cparams(semantics):
    return pltpu.CompilerParams(dimension_semantics=semantics, vmem_limit_bytes=VMEM_LIMIT_BYTES)


def _layer_norm(x, g, b):
    mu = jnp.mean(x, axis=-1, keepdims=True)
    xc = x - mu
    var = jnp.mean(xc * xc, axis=-1, keepdims=True)
    return xc * lax.rsqrt(var + NORM_EPS) * g + b


def _swap16(x):
    lane = lax.broadcasted_iota(jnp.int32, x.shape, x.ndim - 1)
    up = pltpu.roll(x, LANES - 16, axis=x.ndim - 1)
    down = pltpu.roll(x, 16, axis=x.ndim - 1)
    return jnp.where((lane & 16) == 0, up, down)


def _rope_tiles(x, cos, sin):
    tiles = []
    for c in range(x.shape[-1] // LANES):
        xc = x[:, c * LANES:(c + 1) * LANES]
        tiles.append(xc * cos + _swap16(xc) * sin)
    return jnp.concatenate(tiles, axis=-1) if len(tiles) > 1 else tiles[0]


def _store_vt(vt_ref, v_ext, kv_heads, keep=VT_ROWS):
    vt = v_ext.T
    for kv in range(kv_heads):
        vt_ref[kv * keep:(kv + 1) * keep, :] = vt[kv * LANES:kv * LANES + keep].astype(BF16)


def _proj_kernel(x_ref, lng_ref, lnb_ref, w1_ref, bd_ref, ga_ref, cosa_ref, sina_ref,
                 gcq_ref, gckv_ref, wuq_ref, wukv_ref, cosq_ref, sinq_ref, cosk_ref, sink_ref,
                 vones_a_ref, vones_c_ref,
                 *out_refs, apply_ln):
    if apply_ln:
        xn_ref, qa_ref, ka_ref, va_ref, qb_ref, kb_ref, vb_ref, qc_ref, kc_ref, vc_ref = out_refs
    else:
        qa_ref, ka_ref, va_ref, qb_ref, kb_ref, vb_ref, qc_ref, kc_ref, vc_ref = out_refs
    x = x_ref[...]
    if apply_ln:
        x = _layer_norm(x, lng_ref[...], lnb_ref[...])
        xn_ref[...] = x
    xb = x.astype(BF16)
    pc = jnp.dot(xb, w1_ref[:, OFF_C:OFF_C + PC_W], preferred_element_type=F32)
    pa = jnp.dot(xb, w1_ref[:, OFF_A:OFF_B], preferred_element_type=F32)

    cq = pc[:, :C_Q_LORA]
    ckv = pc[:, C_Q_LORA:C_Q_LORA + C_KV_LORA]
    kr = pc[:, C_Q_LORA + C_KV_LORA:]
    cqn = cq * lax.rsqrt(jnp.mean(cq * cq, axis=-1, keepdims=True) + NORM_EPS) * gcq_ref[...]
    ckvn = ckv * lax.rsqrt(jnp.mean(ckv * ckv, axis=-1, keepdims=True) + NORM_EPS) * gckv_ref[...]
    qc = jnp.dot(cqn.astype(BF16), wuq_ref[...], preferred_element_type=F32)
    kv = jnp.dot(ckvn.astype(BF16), wukv_ref[...], preferred_element_type=F32)
    qk = pa[:, :PA_W]
    ms = jnp.dot((qk * qk).astype(BF16), bd_ref[...], preferred_element_type=F32)
    pb = jnp.dot(xb, w1_ref[:, OFF_B:OFF_C], preferred_element_type=F32)

    qc_ref[...] = _rope_tiles(qc, cosq_ref[...], sinq_ref[...]).astype(BF16)
    kpe = _rope_tiles(kr, cosk_ref[...], sink_ref[...])
    hw = C_HEADS * LANES
    kc_ref[...] = (kv[:, :hw] + jnp.concatenate([kpe] * C_HEADS, axis=-1)).astype(BF16)
    _store_vt(vc_ref, kv[:, hw:] + vones_c_ref[...], C_HEADS)

    qk = qk * lax.rsqrt(ms + NORM_EPS) * ga_ref[...]
    qk = _rope_tiles(qk, cosa_ref[...], sina_ref[...])
    nq = A_HEADS * HEAD_DIM
    qa_ref[...] = qk[:, :nq].astype(BF16)
    kk = qk[:, nq:]
    low = lax.broadcasted_iota(jnp.int32, kk.shape, 1) < HEAD_DIM
    ka_ref[...] = jnp.concatenate([jnp.where(low, kk, 0.0),
                                   jnp.where(low, pltpu.roll(kk, HEAD_DIM, axis=1), 0.0)],
                                  axis=-1).astype(BF16)
    _store_vt(va_ref, pa[:, PA_W:] + vones_a_ref[...], A_KV_HEADS)

    nqb = B_HEADS * HEAD_DIM
    nkb = B_KV_HEADS * HEAD_DIM
    qb_ref[...] = pb[:, :nqb].astype(BF16)
    kb_ref[...] = pb[:, nqb:nqb + nkb].astype(BF16)
    _store_vt(vb_ref, pb[:, nqb + nkb:], B_KV_HEADS, keep=HEAD_DIM)


def _proj_call(x, lng, lnb, w1, bd, ga, cosa, sina, gcq, gckv, wuq, wukv, cosq, sinq, cosk, sink,
               vones_a, vones_c, *, seq, apply_ln):
    n = x.shape[0]
    tm = PROJ_ROWS
    nsb = seq // tm
    row = lambda w: pl.BlockSpec((tm, w), lambda i: (i, 0))
    full = lambda a: pl.BlockSpec(a.shape, lambda i: (0,) * a.ndim, pipeline_mode=pl.Buffered(1))
    tab = pl.BlockSpec((tm, LANES), lambda i: (i % nsb, 0))
    out_w = [A_HEADS * HEAD_DIM, A_KV_HEADS * LANES, -A_KV_HEADS * VT_ROWS,
             B_HEADS * HEAD_DIM, B_KV_HEADS * HEAD_DIM, -B_KV_HEADS * HEAD_DIM,
             C_HEADS * LANES, C_HEADS * LANES, -C_HEADS * VT_ROWS]
    out_shape = [jax.ShapeDtypeStruct((n, w) if w > 0 else (-w, n), BF16) for w in out_w]
    out_specs = [row(w) if w > 0 else pl.BlockSpec((-w, tm), lambda i: (0, i)) for w in out_w]
    if apply_ln:
        out_shape = [jax.ShapeDtypeStruct((n, D_MODEL), F32)] + out_shape
        out_specs = [row(D_MODEL)] + out_specs
    return pl.pallas_call(
        functools.partial(_proj_kernel, apply_ln=apply_ln),
        out_shape=out_shape,
        grid=(n // tm,),
        in_specs=[row(D_MODEL), full(lng), full(lnb), full(w1), full(bd), full(ga), tab, tab,
                  full(gcq), full(gckv), full(wuq), full(wukv), tab, tab, tab, tab,
                  full(vones_a), full(vones_c)],
        out_specs=out_specs,
        compiler_params=_cparams(("parallel",)),
        name="proj_ln" if apply_ln else "proj",
    )(x, lng, lnb, w1, bd, ga, cosa, sina, gcq, gckv, wuq, wukv, cosq, sinq, cosk, sink,
      vones_a, vones_c)


def _dense_attn_kernel(q_ref, k_ref, vt_ref, o_ref, *, heads, kv_heads, dq):
    group = heads // kv_heads
    seq = k_ref.shape[0]
    tq = q_ref.shape[0]
    n_chunks = seq // ATTN_KEY_CHUNK
    units = [(h0 + dh, c) for h0 in range(0, heads, ATTN_HEADS_INTERLEAVED) for c in range(n_chunks)
             for dh in range(ATTN_HEADS_INTERLEAVED)]

    def scores_t(h, c):
        kv = h // group
        qh = q_ref[:, h * dq:(h + 1) * dq]
        kc = k_ref[c * ATTN_KEY_CHUNK:(c + 1) * ATTN_KEY_CHUNK, kv * LANES:(kv + 1) * LANES]
        if dq < LANES:
            kc = kc[:, :dq]
        return lax.dot_general(kc, qh, (((1,), (1,)), ((), ())), preferred_element_type=F32)

    ahead = 2
    st = {u: scores_t(*u) for u in units[:ahead]}
    outs = [None] * heads
    m, acc = {}, {}
    for idx, (h, c) in enumerate(units):
        kv = h // group
        s = st.pop((h, c))
        if idx + ahead < len(units):
            st[units[idx + ahead]] = scores_t(*units[idx + ahead])
        cmax = jnp.max(s, axis=0, keepdims=True)
        m_new = cmax if c == 0 else jnp.maximum(m[h], cmax)
        pt = jnp.exp2(s - m_new).astype(BF16)
        vtc = vt_ref[kv * VT_ROWS:(kv + 1) * VT_ROWS, c * ATTN_KEY_CHUNK:(c + 1) * ATTN_KEY_CHUNK]
        pv = jnp.dot(vtc, pt, preferred_element_type=F32)
        acc[h] = pv if c == 0 else acc[h] * jnp.exp2(m[h] - m_new) + pv
        m[h] = m_new
        if c == n_chunks - 1:
            outs[h] = acc[h][:HEAD_DIM] / acc[h][HEAD_DIM:HEAD_DIM + 1]
    o_ref[...] = jnp.concatenate(outs, axis=0).T.astype(o_ref.dtype)


def _dense_attn_call(q, k, vt, *, batch, seq, heads, kv_heads, dq, name):
    n = q.shape[0]
    tq = ATTN_Q_ROWS
    nq = seq // tq
    return pl.pallas_call(
        functools.partial(_dense_attn_kernel, heads=heads, kv_heads=kv_heads, dq=dq),
        out_shape=jax.ShapeDtypeStruct((n, heads * HEAD_DIM), BF16),
        grid=(batch, nq),
        in_specs=[pl.BlockSpec((tq, heads * dq), lambda b, i: (b * nq + i, 0)),
                  pl.BlockSpec((seq, kv_heads * LANES), lambda b, i: (b, 0)),
                  pl.BlockSpec((kv_heads * VT_ROWS, seq), lambda b, i: (0, b))],
        out_specs=pl.BlockSpec((tq, heads * HEAD_DIM), lambda b, i: (b * nq + i, 0)),
        compiler_params=_cparams(("parallel", "parallel")),
        name=name,
    )(q, k, vt)


def _win_attn_kernel(q_ref, k_ref, vt_ref, bias_ref, o_ref, *, heads, kv_heads, nb):
    group = heads // kv_heads
    tq = WIN_ROWS
    span = 3 * tq
    units = [(j, kv) for j in range(WIN_BLOCKS_PER_STEP) for kv in range(kv_heads)]

    def logits_t(j, kv):
        i = pl.program_id(1) * WIN_BLOCKS_PER_STEP + j
        qs = jnp.concatenate(
            [q_ref[j * tq:(j + 1) * tq, (kv * group + g) * HEAD_DIM:(kv * group + g + 1) * HEAD_DIM]
             for g in range(group)], axis=0)
        bias = bias_ref[kv]
        logit_chunks, vt_chunks = [], []
        for c in range(3):
            blk = i + (c - 1)
            start = pl.multiple_of(jnp.clip(blk, 0, nb - 1) * tq, tq)
            kc = k_ref[pl.ds(start, tq), kv * HEAD_DIM:(kv + 1) * HEAD_DIM]
            vt_chunks.append(vt_ref[kv * HEAD_DIM:(kv + 1) * HEAD_DIM, pl.ds(start, tq)])
            sc = lax.dot_general(kc, qs, (((1,), (1,)), ((), ())), preferred_element_type=F32)
            sc = sc + bias[c * tq:(c + 1) * tq]
            if c != 1:
                sc = jnp.where(jnp.logical_and(blk >= 0, blk <= nb - 1), sc, NEG_INF)
            logit_chunks.append(sc)
        logit_chunks.append(bias[span:])
        return jnp.concatenate(logit_chunks, axis=0), jnp.concatenate(vt_chunks, axis=1)

    ahead = 2
    pending = {u: logits_t(*u) for u in units[:ahead]}
    outs = []
    for idx, (j, kv) in enumerate(units):
        logits, vt = pending.pop((j, kv))
        if idx + ahead < len(units):
            pending[units[idx + ahead]] = logits_t(*units[idx + ahead])
        m = jnp.max(logits, axis=0, keepdims=True)
        pr = jnp.exp2(logits - m)
        denom = jnp.sum(pr, axis=0, keepdims=True)
        ot = jnp.dot(vt, pr[:span].astype(BF16), preferred_element_type=F32) / denom
        outs += [ot[:, g * tq:(g + 1) * tq] for g in range(group)]
        if kv == kv_heads - 1:
            o_ref[j * tq:(j + 1) * tq, :] = jnp.concatenate(outs, axis=0).T.astype(o_ref.dtype)
            outs = []


def _win_attn_call(q, k, vt, bias, *, batch, seq, heads, kv_heads):
    n = q.shape[0]
    ts = WIN_ROWS * WIN_BLOCKS_PER_STEP
    nq = seq // ts
    return pl.pallas_call(
        functools.partial(_win_attn_kernel, heads=heads, kv_heads=kv_heads, nb=seq // WIN_ROWS),
        out_shape=jax.ShapeDtypeStruct((n, heads * HEAD_DIM), BF16),
        grid=(batch, nq),
        in_specs=[pl.BlockSpec((ts, heads * HEAD_DIM), lambda b, i: (b * nq + i, 0)),
                  pl.BlockSpec((seq, kv_heads * HEAD_DIM), lambda b, i: (b, 0)),
                  pl.BlockSpec((kv_heads * HEAD_DIM, seq), lambda b, i: (0, b)),
                  pl.BlockSpec(bias.shape, lambda b, i: (0, 0, 0))],
        out_specs=pl.BlockSpec((ts, heads * HEAD_DIM), lambda b, i: (b * nq + i, 0)),
        compiler_params=_cparams(("parallel", "parallel")),
        name="win_attn",
    )(q, k, vt, bias)


def _store_row_tiles(ref, x):
    rows = x.shape[0]
    for c in range(ROW_TILES):
        ref[pl.ds(c, rows, stride=ROW_TILES), :] = x[:, c * LANES:(c + 1) * LANES]


def _load_row_tiles(ref):
    rows = ref.shape[0] // ROW_TILES
    return jnp.concatenate([ref[pl.ds(c, rows, stride=ROW_TILES), :] for c in range(ROW_TILES)], axis=-1)


def _merge_kernel(x_ref, oa_ref, ob_ref, oc_ref, wg_ref, wa_ref, wb_ref, wc_ref, wo_ref,
                  g_ref, b_ref, rwt_ref, rb_ref, tri_ref,
                  x1_ref, e_ref, w_ref, r_ref, cnt_ref, carry_ref, *, alpha):
    step = pl.program_id(0)

    @pl.when(step == 0)
    def _():
        carry_ref[...] = jnp.zeros_like(carry_ref)

    x = x_ref[...]
    xb = x.astype(BF16)
    d = D_MODEL
    mixers = ((oa_ref[...], wa_ref), (ob_ref[...], wb_ref), (oc_ref[...], wc_ref))
    tiles = []
    for c0 in range(0, d, MERGE_COL_TILE):
        acc = None
        for idx, (o, wbr_ref) in enumerate(mixers):
            gate = jax.nn.sigmoid(jnp.dot(xb, wg_ref[:, idx * d + c0:idx * d + c0 + MERGE_COL_TILE],
                                          preferred_element_type=F32))
            br = jnp.dot(o, wbr_ref[:, c0:c0 + MERGE_COL_TILE], preferred_element_type=F32)
            acc = gate * br if acc is None else acc + gate * br
        tiles.append(acc.astype(BF16))
    merged = jnp.concatenate(tiles, axis=-1)
    mix = jnp.dot(merged, wo_ref[...], preferred_element_type=F32)
    x1 = _layer_norm(alpha * x + mix, g_ref[...], b_ref[...])
    _store_row_tiles(x1_ref, x1)

    logits = lax.dot_general(rwt_ref[...], x1, (((1,), (1,)), ((), ())),
                             preferred_element_type=F32, precision=lax.Precision.HIGHEST)
    scores = jax.nn.sigmoid(logits)
    biased = scores + rb_ref[...]
    shape = biased.shape
    rows = lax.broadcasted_iota(jnp.int32, shape, 0)
    sub = lax.broadcasted_iota(jnp.int32, (EXPERTS_PER_GROUP, shape[1]), 0)
    best, sel = None, None
    for g in range(N_GROUPS):
        blk = biased[g * EXPERTS_PER_GROUP:(g + 1) * EXPERTS_PER_GROUP]
        m1 = jnp.max(blk, axis=0, keepdims=True)
        i1 = jnp.min(jnp.where(blk == m1, sub, EXPERTS_PER_GROUP), axis=0, keepdims=True)
        m2 = jnp.max(jnp.where(sub == i1, -jnp.inf, blk), axis=0, keepdims=True)
        gs = m1 + m2
        if best is None:
            best, sel = gs, jnp.zeros_like(i1)
        else:
            better = gs > best
            sel = jnp.where(better, g, sel)
            best = jnp.where(better, gs, best)
    in_group = (rows >> 2) == sel
    bm = jnp.where(in_group, biased, -jnp.inf)
    t1 = jnp.max(bm, axis=0, keepdims=True)
    e0 = jnp.min(jnp.where(bm == t1, rows, N_EXPERTS), axis=0, keepdims=True)
    bm2 = jnp.where(rows == e0, -jnp.inf, bm)
    t2 = jnp.max(bm2, axis=0, keepdims=True)
    e1 = jnp.min(jnp.where(bm2 == t2, rows, N_EXPERTS), axis=0, keepdims=True)
    hot0 = rows == e0
    hot1 = rows == e1
    s0 = jnp.sum(jnp.where(hot0, scores, 0.0), axis=0, keepdims=True)
    s1 = jnp.sum(jnp.where(hot1, scores, 0.0), axis=0, keepdims=True)
    tot = s0 + s1
    e_ref[...] = jnp.concatenate([e0, e1], axis=0)
    w_ref[...] = jnp.concatenate([s0 / tot, s1 / tot], axis=0)

    hot = jnp.where(jnp.logical_or(hot0, hot1), 1.0, 0.0)
    prefix = jnp.dot(hot.astype(BF16), tri_ref[...], preferred_element_type=F32) + carry_ref[:, 0:1]
    r0 = jnp.sum(jnp.where(hot0, prefix, 0.0), axis=0, keepdims=True)
    r1 = jnp.sum(jnp.where(hot1, prefix, 0.0), axis=0, keepdims=True)
    r_ref[...] = jnp.concatenate([r0, r1], axis=0).astype(jnp.int32)
    carry_ref[...] = carry_ref[...] + jnp.sum(hot, axis=1, keepdims=True)
    cnt_ref[...] = carry_ref[...]


def _merge_call(x, oa, ob, oc, wg, wa, wb, wc, wo, g, b, rwt, rb, tri, *, alpha):
    n = x.shape[0]
    tm = MERGE_ROWS
    row = lambda w: pl.BlockSpec((tm, w), lambda i: (i, 0))
    full = lambda a: pl.BlockSpec(a.shape, lambda i: (0,) * a.ndim, pipeline_mode=pl.Buffered(1))
    tok = pl.BlockSpec((TOP_K, tm), lambda i: (0, i))
    slabs = pl.BlockSpec((tm * ROW_TILES, LANES), lambda i: (i, 0))
    return pl.pallas_call(
        functools.partial(_merge_kernel, alpha=alpha),
        out_shape=[jax.ShapeDtypeStruct((n * ROW_TILES, LANES), F32),
                   jax.ShapeDtypeStruct((TOP_K, n), jnp.int32),
                   jax.ShapeDtypeStruct((TOP_K, n), F32),
                   jax.ShapeDtypeStruct((TOP_K, n), jnp.int32),
                   jax.ShapeDtypeStruct((N_EXPERTS, LANES), F32)],
        grid=(n // tm,),
        in_specs=[row(D_MODEL), row(oa.shape[1]), row(ob.shape[1]), row(oc.shape[1]),
                  full(wg), full(wa), full(wb), full(wc), full(wo), full(g), full(b),
                  full(rwt), full(rb), full(tri)],
        out_specs=[slabs, tok, tok, tok,
                   pl.BlockSpec((N_EXPERTS, LANES), lambda i: (0, 0))],
        scratch_shapes=[pltpu.VMEM((N_EXPERTS, LANES), F32)],
        compiler_params=_cparams(("arbitrary",)),
        name="merge",
    )(x, oa, ob, oc, wg, wa, wb, wc, wo, g, b, rwt, rb, tri)


def _row_slab(ref, first):
    if not isinstance(first, int):
        first = pl.multiple_of(first, ROW_TILES)
    return ref.at[pl.ds(first, ROW_TILES)]


def _dispatch_kernel(pad_first_ref, pad_rows_ref, tail_ref, pos_ref, x_ref, dst_hbm, zero_ref, sem, zero_sem):
    tb = DISPATCH_ROWS

    @pl.when(pl.program_id(0) == 0)
    def _():
        zero_ref[...] = jnp.zeros_like(zero_ref)

        def for_each_fill(act):
            for e in range(N_EXPERTS):
                first, n = pad_first_ref[e], pad_rows_ref[e]
                for bit in reversed(range(EXPERT_ROWS.bit_length() - 1)):
                    run = (1 << bit) * ROW_TILES
                    has = (n & (1 << bit)) != 0

                    @pl.when(has)
                    def _(first=first, run=run):
                        act(pltpu.make_async_copy(zero_ref.at[pl.ds(0, run)],
                                                  dst_hbm.at[pl.ds(pl.multiple_of(first, ROW_TILES), run)],
                                                  zero_sem))
                    first = first + jnp.where(has, run, 0)
            tile = EXPERT_ROWS * ROW_TILES
            for k in range(N_EXPERTS):
                @pl.when(k < tail_ref[1])
                def _(k=k):
                    act(pltpu.make_async_copy(
                        zero_ref, dst_hbm.at[pl.ds(pl.multiple_of((tail_ref[0] + k) * tile, tile), tile)], zero_sem))

        for_each_fill(lambda copy: copy.start())
        for_each_fill(lambda copy: copy.wait())

    def start(g, carry):
        for u in range(DMA_UNROLL):
            t = g * DMA_UNROLL + u
            for j in range(TOP_K):
                pltpu.make_async_copy(_row_slab(x_ref, t * ROW_TILES),
                                      _row_slab(dst_hbm, pos_ref[0, 0, j * tb + t]),
                                      sem).start(priority=(u + j) % 2)
        return carry

    lax.fori_loop(0, tb // DMA_UNROLL, start, 0)
    for j in range(TOP_K):
        pltpu.make_async_copy(x_ref, dst_hbm.at[pl.ds(0, tb * ROW_TILES)], sem).wait()


def _dispatch_call(pad_first, pad_rows, tail, pos_flat, x1, *, p_rows):
    nsteps = pos_flat.shape[0]
    tb = DISPATCH_ROWS
    grid_spec = pltpu.PrefetchScalarGridSpec(
        num_scalar_prefetch=3,
        grid=(nsteps,),
        in_specs=[pl.BlockSpec((1, 1, TOP_K * tb), lambda i, *_: (i, 0, 0), memory_space=pltpu.SMEM),
                  pl.BlockSpec((tb * ROW_TILES, LANES), lambda i, *_: (i, 0))],
        out_specs=pl.BlockSpec(memory_space=pl.ANY),
        scratch_shapes=[pltpu.VMEM((EXPERT_ROWS * ROW_TILES, LANES), F32),
                        pltpu.SemaphoreType.DMA(()), pltpu.SemaphoreType.DMA(())],
    )
    return pl.pallas_call(
        _dispatch_kernel,
        out_shape=jax.ShapeDtypeStruct((p_rows * ROW_TILES, LANES), F32),
        grid_spec=grid_spec,
        compiler_params=pltpu.CompilerParams(dimension_semantics=("arbitrary",), has_side_effects=True,
                                             vmem_limit_bytes=VMEM_LIMIT_BYTES),
        name="dispatch",
    )(pad_first, pad_rows, tail, pos_flat, x1)


def _expert_kernel(te_ref, nu_ref, xs_ref, wg_ref, wu_ref, wd_ref, ys_ref, wgu_bf, wd_bf):
    i = pl.program_id(0)

    @pl.when(jnp.logical_or(i == 0, te_ref[i] != te_ref[jnp.maximum(i - 1, 0)]))
    def _():
        wgu_bf[:, :D_FF] = wg_ref[0, 0].astype(BF16)
        wgu_bf[:, D_FF:] = wu_ref[0, 0].astype(BF16)
        wd_bf[...] = wd_ref[0, 0].astype(BF16)

    @pl.when(i < nu_ref[0])
    def _():
        xb = _load_row_tiles(xs_ref).astype(BF16)
        gu = [(jnp.dot(xb, wgu_bf[:, c0:c0 + EXPERT_COL_TILE], preferred_element_type=F32),
               jnp.dot(xb, wgu_bf[:, D_FF + c0:D_FF + c0 + EXPERT_COL_TILE], preferred_element_type=F32))
              for c0 in range(0, D_FF, EXPERT_COL_TILE)]
        hid = [((gate * jax.nn.sigmoid(gate)) * up).astype(BF16) for gate, up in gu]
        y = jnp.dot(jnp.concatenate(hid, axis=-1), wd_bf[...], preferred_element_type=F32)
        _store_row_tiles(ys_ref, y)

    @pl.when(i >= nu_ref[0])
    def _():
        ys_ref[...] = jnp.zeros_like(ys_ref)


def _expert_call(tile_expert, n_used, xs, w_gate, w_up, w_down, *, layer):
    p = xs.shape[0] // ROW_TILES
    tm = EXPERT_ROWS
    wspec = lambda r, c: pl.BlockSpec((1, 1, r, c), lambda i, te, nu: (layer, te[i], 0, 0))
    grid_spec = pltpu.PrefetchScalarGridSpec(
        num_scalar_prefetch=2,
        grid=(p // tm,),
        in_specs=[pl.BlockSpec((tm * ROW_TILES, LANES), lambda i, te, nu: (i, 0)),
                  wspec(D_MODEL, D_FF), wspec(D_MODEL, D_FF), wspec(D_FF, D_MODEL)],
        out_specs=pl.BlockSpec((tm * ROW_TILES, LANES), lambda i, te, nu: (i, 0)),
        scratch_shapes=[pltpu.VMEM((D_MODEL, 2 * D_FF), BF16), pltpu.VMEM((D_FF, D_MODEL), BF16)],
    )
    return pl.pallas_call(
        _expert_kernel,
        out_shape=jax.ShapeDtypeStruct((p * ROW_TILES, LANES), F32),
        grid_spec=grid_spec,
        compiler_params=_cparams(("arbitrary",)),
        name="experts",
    )(tile_expert, n_used, xs, w_gate, w_up, w_down)


def _combine_kernel(pos_ref, nxt_ref, ys_hbm, x_ref, w_ref, g_ref, b_ref, o_ref, buf_ref, sem, *, alpha):
    tc = COMBINE_ROWS
    i = pl.program_id(0)
    slot = i % 2

    def row_copy(idx_ref, to_slot, t, j):
        return pltpu.make_async_copy(_row_slab(ys_hbm, idx_ref[0, 0, j * tc + t]),
                                     _row_slab(buf_ref.at[to_slot, j], t * ROW_TILES), sem.at[to_slot])

    def wait_slot(s):
        for j in range(TOP_K):
            pltpu.make_async_copy(ys_hbm.at[pl.ds(0, tc * ROW_TILES)], buf_ref.at[s, j], sem.at[s]).wait()

    @pl.when(i == 0)
    def _():
        def start(g, carry):
            for u in range(DMA_UNROLL):
                for j in range(TOP_K):
                    row_copy(pos_ref, 0, g * DMA_UNROLL + u, j).start(priority=(u + j) % 2)
            return carry
        lax.fori_loop(0, tc // DMA_UNROLL, start, 0)

    for t in range(tc):
        for j in range(TOP_K):
            row_copy(nxt_ref, 1 - slot, t, j).start(priority=(t + j) % 2)
    wait_slot(slot)
    w = w_ref[...]
    ffn = (w[:, 0:1] * _load_row_tiles(buf_ref.at[slot, 0])
           + w[:, 1:2] * _load_row_tiles(buf_ref.at[slot, 1]))
    o_ref[...] = _layer_norm(alpha * _load_row_tiles(x_ref) + ffn, g_ref[...], b_ref[...])

    @pl.when(i == pl.num_programs(0) - 1)
    def _():
        wait_slot(1 - slot)


def _combine_call(pos_flat, ys, x1, wts_t, g, b, *, alpha):
    n = x1.shape[0] // ROW_TILES
    tc = COMBINE_ROWS
    nsteps = n // tc
    idx = lambda shift: pl.BlockSpec((1, 1, TOP_K * tc), lambda i: (jnp.minimum(i + shift, nsteps - 1), 0, 0),
                                     memory_space=pltpu.SMEM)
    return pl.pallas_call(
        functools.partial(_combine_kernel, alpha=alpha),
        out_shape=jax.ShapeDtypeStruct((n, D_MODEL), F32),
        grid=(nsteps,),
        in_specs=[idx(0), idx(1),
                  pl.BlockSpec(memory_space=pl.ANY),
                  pl.BlockSpec((tc * ROW_TILES, LANES), lambda i: (i, 0)),
                  pl.BlockSpec((tc, TOP_K), lambda i: (i, 0)),
                  pl.BlockSpec(g.shape, lambda i: (0, 0)),
                  pl.BlockSpec(b.shape, lambda i: (0, 0))],
        out_specs=pl.BlockSpec((tc, D_MODEL), lambda i: (i, 0)),
        scratch_shapes=[pltpu.VMEM((2, TOP_K, tc * ROW_TILES, LANES), F32),
                        pltpu.SemaphoreType.DMA((2,))],
        compiler_params=_cparams(("arbitrary",)),
        name="combine",
    )(pos_flat, pos_flat, ys, x1, wts_t, g, b)


def _t5_bucket(rel):
    nb = N_BUCKETS // 2
    ret = (rel > 0).astype(np.int32) * nb
    n = np.abs(rel)
    max_exact = nb // 2
    large = max_exact + (np.log(np.maximum(n, 1) / max_exact)
                         / math.log(MAX_DISTANCE / max_exact) * (nb - max_exact)).astype(np.int32)
    large = np.minimum(large, nb - 1)
    return (ret + np.where(n < max_exact, n, large)).astype(np.int32)


def _rope_tables(seq):
    def cs(pos, dim):
        inv = ROPE_THETA ** (-(jnp.arange(0, dim, 2, dtype=F32) / dim))
        ang = pos.astype(F32)[:, None] * inv[None, :]
        return jnp.cos(ang), jnp.sin(ang)

    t = jnp.arange(seq)
    cr, sr = cs(t // GRID_W, HEAD_DIM // 2)
    cc, sc = cs(t % GRID_W, HEAD_DIM // 2)
    cos_head = jnp.concatenate([cr, cr, cc, cc], axis=-1)
    sin_head = jnp.concatenate([-sr, sr, -sc, sc], axis=-1)
    cos_a = jnp.concatenate([cos_head, cos_head], axis=-1)
    sin_a = jnp.concatenate([sin_head, sin_head], axis=-1)
    cs_, ss_ = cs(t, C_ROPE)
    ones = jnp.ones((seq, C_NOPE), F32)
    pad1 = jnp.ones((seq, LANES - C_NOPE - C_ROPE), F32)
    cos_k = jnp.concatenate([ones, cs_, cs_, pad1], axis=-1)
    sin_k = jnp.concatenate([0 * ones, -ss_, ss_, 0 * pad1], axis=-1)
    scale = (C_NOPE + C_ROPE) ** -0.5 * LOG2E
    return cos_a, sin_a, cos_k * scale, sin_k * scale, cos_k, sin_k


def _window_bias(rpb_table, b_sink_l):
    span = WIN_ROWS + 2 * WINDOW
    rel = np.arange(span)[None, :] - WINDOW - np.arange(WIN_ROWS)[:, None]
    band = jnp.asarray(np.abs(rel) <= WINDOW)
    onehot = jnp.asarray(_t5_bucket(rel)[:, :, None] == np.arange(N_BUCKETS), F32)
    bias = jnp.einsum("qsb,bh->qsh", onehot, rpb_table.astype(F32),
                      precision=lax.Precision.HIGHEST)
    bias = jnp.where(band[:, :, None], bias * LOG2E, NEG_INF)
    bias = jnp.transpose(bias, (1, 2, 0))
    sink = jnp.full((8, B_HEADS, WIN_ROWS), NEG_INF, F32)
    sink = sink.at[0].set(jnp.broadcast_to(b_sink_l.astype(F32)[:, None] * LOG2E, (B_HEADS, WIN_ROWS)))
    ext = jnp.concatenate([bias, sink], axis=0)
    group = B_HEADS // B_KV_HEADS
    ext = ext.reshape(span + 8, B_KV_HEADS, group * WIN_ROWS)
    return jnp.transpose(ext, (1, 0, 2))


def _pad_cols(w, groups, width, total):
    k = w.shape[0]
    w = w.reshape(k, groups, width)
    w = jnp.pad(w, ((0, 0), (0, 0), (0, total - width)))
    return w.reshape(k, groups * total)


def _layer_params(l, w_in, a_q_norm_g, a_k_norm_g, c_q_norm_g, c_kv_norm_g, c_w_uq, c_w_ukv,
                  w_branch_a, w_branch_b, w_branch_c, w_o):
    w = w_in[l]
    splits = np.cumsum([0, A_HEADS * HEAD_DIM, A_KV_HEADS * HEAD_DIM, A_KV_HEADS * HEAD_DIM,
                        B_HEADS * HEAD_DIM, B_KV_HEADS * HEAD_DIM, B_KV_HEADS * HEAD_DIM,
                        C_Q_LORA, C_KV_LORA, C_ROPE, D_MODEL, D_MODEL, D_MODEL])
    part = [w[:, splits[i]:splits[i + 1]] for i in range(12)]
    qa, ka, va, qb, kb, vb, cq, ckv, kr, ga, gb, gc = part
    va_e = _pad_cols(va, A_KV_HEADS, HEAD_DIM, LANES)
    kr_p = jnp.pad(kr, ((0, 0), (C_NOPE, LANES - C_NOPE - C_ROPE)))
    vb_e = _pad_cols(vb, B_KV_HEADS, HEAD_DIM, LANES)
    w1 = jnp.concatenate([qa, ka, va_e, qb * (HEAD_DIM ** -0.5 * LOG2E), kb, vb_e, cq, ckv, kr_p],
                         axis=1).astype(BF16)
    gate_a = jnp.concatenate([jnp.tile(a_q_norm_g[l], A_HEADS) * (HEAD_DIM ** -0.5 * LOG2E),
                              jnp.tile(a_k_norm_g[l], A_KV_HEADS)])[None, :].astype(F32)
    wuq = _pad_cols(c_w_uq[l], C_HEADS, C_NOPE + C_ROPE, LANES).astype(BF16)
    ukv = c_w_ukv[l].reshape(C_KV_LORA, C_HEADS, C_NOPE + C_V)
    wuk = jnp.pad(ukv[:, :, :C_NOPE], ((0, 0), (0, 0), (0, LANES - C_NOPE))).reshape(C_KV_LORA, C_HEADS * LANES)
    wuv = jnp.pad(ukv[:, :, C_NOPE:], ((0, 0), (0, 0), (0, LANES - C_V))).reshape(C_KV_LORA, C_HEADS * LANES)
    wukv = jnp.concatenate([wuk, wuv], axis=1).astype(BF16)
    wg = jnp.concatenate([ga, gb, gc], axis=1).astype(BF16)
    return dict(w1=w1, gate_a=gate_a, gcq=c_q_norm_g[l][None, :].astype(F32),
                gckv=c_kv_norm_g[l][None, :].astype(F32), wuq=wuq, wukv=wukv, wg=wg,
                wa=w_branch_a[l].astype(BF16), wb=w_branch_b[l].astype(BF16),
                wc=w_branch_c[l].astype(BF16), wo=w_o[l].astype(BF16))


def kernel(x, ln_in_g, ln_in_b, w_in, a_q_norm_g, a_k_norm_g, b_sink, rpb_table, c_q_norm_g, c_kv_norm_g,
           c_w_uq, c_w_ukv, w_branch_a, w_branch_b, w_branch_c, w_o, ln1_g, ln1_b, router_w, router_bias,
           w_gate, w_up, w_down, ln2_g, ln2_b):
    batch, seq, d = x.shape
    depth = w_in.shape[0]
    n = batch * seq
    alpha = (2 * depth) ** 0.25
    assert d == D_MODEL and seq % PROJ_ROWS == 0 and seq % ATTN_Q_ROWS == 0
    assert seq % (WIN_ROWS * WIN_BLOCKS_PER_STEP) == 0
    assert n % DISPATCH_ROWS == 0 and n % MERGE_ROWS == 0 and n % COMBINE_ROWS == 0

    cos_a, sin_a, cos_q, sin_q, cos_k, sin_k = _rope_tables(seq)
    seg = np.arange(PA_W) // HEAD_DIM
    bd = jnp.asarray((seg[:, None] == seg[None, :]).astype(np.float32) / HEAD_DIM, BF16)
    ones_pat = np.zeros((1, LANES), np.float32)
    ones_pat[0, HEAD_DIM:] = 1.0
    vones_a = jnp.asarray(np.tile(ones_pat, (1, A_KV_HEADS)))
    vones_c = jnp.asarray(np.tile(ones_pat, (1, C_HEADS)))
    tri = jnp.asarray(np.triu(np.ones((MERGE_ROWS, MERGE_ROWS), np.float32), k=1), BF16)
    rwt = router_w.astype(F32).T
    rb = router_bias.astype(F32)[:, None]
    lng = ln_in_g.astype(F32)[None, :]
    lnb = ln_in_b.astype(F32)[None, :]

    tme = EXPERT_ROWS
    n_tiles = (TOP_K * n) // tme + N_EXPERTS
    p_rows = n_tiles * tme

    h = x.reshape(n, d).astype(F32)
    for l in range(depth):
        prm = _layer_params(l, w_in, a_q_norm_g, a_k_norm_g, c_q_norm_g, c_kv_norm_g, c_w_uq, c_w_ukv,
                            w_branch_a, w_branch_b, w_branch_c, w_o)
        outs = _proj_call(h, lng, lnb, prm["w1"], bd, prm["gate_a"], cos_a, sin_a, prm["gcq"], prm["gckv"],
                          prm["wuq"], prm["wukv"], cos_q, sin_q, cos_k, sin_k, vones_a, vones_c,
                          seq=seq, apply_ln=(l == 0))
        if l == 0:
            h, *outs = outs
        qa, ka, va, qb, kb, vb, qc, kc, vc = outs
        oa = _dense_attn_call(qa, ka, va, batch=batch, seq=seq, heads=A_HEADS, kv_heads=A_KV_HEADS,
                              dq=HEAD_DIM, name="attn_a")
        bias = _window_bias(rpb_table, b_sink[l])
        ob = _win_attn_call(qb, kb, vb, bias, batch=batch, seq=seq, heads=B_HEADS, kv_heads=B_KV_HEADS)
        oc = _dense_attn_call(qc, kc, vc, batch=batch, seq=seq, heads=C_HEADS, kv_heads=C_HEADS,
                              dq=LANES, name="attn_c")
        x1, e_idx, wts, rank, cnt = _merge_call(
            h, oa, ob, oc, prm["wg"], prm["wa"], prm["wb"], prm["wc"], prm["wo"],
            ln1_g[l].astype(F32)[None, :], ln1_b[l].astype(F32)[None, :], rwt, rb, tri, alpha=alpha)

        counts = cnt[:, 0].astype(jnp.int32)
        padded = ((counts + tme - 1) // tme) * tme
        ends = jnp.cumsum(padded)
        offs = ends - padded
        pos = rank
        for e in range(N_EXPERTS):
            pos = pos + jnp.where(e_idx == e, offs[e], 0)
        tile_start = jnp.arange(n_tiles, dtype=jnp.int32) * tme
        tile_expert = jnp.minimum(jnp.sum(tile_start[:, None] >= ends[None, :], axis=1), N_EXPERTS - 1)
        tile_expert = tile_expert.astype(jnp.int32)
        n_used = (ends[-1:] // tme).astype(jnp.int32)

        first_row = pos * ROW_TILES

        def per_step(rows):
            return first_row.reshape(TOP_K, n // rows, rows).transpose(1, 0, 2).reshape(n // rows, 1, TOP_K * rows)

        pad_first = ((offs + counts) * ROW_TILES).astype(jnp.int32)
        pad_rows = (padded - counts).astype(jnp.int32)
        tail = jnp.concatenate([n_used, n_tiles - n_used]).astype(jnp.int32)
        xs = _dispatch_call(pad_first, pad_rows, tail, per_step(DISPATCH_ROWS), x1, p_rows=p_rows)
        ys = _expert_call(tile_expert, n_used, xs, w_gate.astype(F32), w_up.astype(F32), w_down.astype(F32),
                          layer=l)
        pos_c = per_step(COMBINE_ROWS)
        h = _combine_call(pos_c, ys, x1, wts.T, ln2_g[l].astype(F32)[None, :], ln2_b[l].astype(F32)[None, :],
                          alpha=alpha)
    return h.reshape(batch, seq, d).astype(x.dtype)
```

```python
import functools
import math

import numpy as np
import jax
import jax.numpy as jnp
from jax import lax
from jax.experimental import pallas as pl
from jax.experimental.pallas import tpu as pltpu

F32 = jnp.float32
BF16 = jnp.bfloat16

D_MODEL = 1024
GRID_W = 64
HEAD_DIM = 64
A_HEADS, A_KV_HEADS = 8, 2
B_HEADS, B_KV_HEADS = 8, 2
WINDOW = 128
C_HEADS = 8
C_NOPE, C_ROPE, C_V = 64, 32, 64
C_Q_LORA, C_KV_LORA = 384, 256
N_BUCKETS, MAX_DISTANCE = 32, 128
N_EXPERTS, N_GROUPS, TOP_K = 16, 4, 2
EXPERTS_PER_GROUP = N_EXPERTS // N_GROUPS
D_FF = 512
ROPE_THETA = 10000.0
NORM_EPS = 1e-6
NEG_INF = -1e30

LANES = 128
ROW_TILES = D_MODEL // LANES
BF16_SUBLANES = 16
VT_ROWS = HEAD_DIM + BF16_SUBLANES
LOG2E = math.log2(math.e)
VMEM_LIMIT_BYTES = 48 * 1024 * 1024

PROJ_ROWS = 512
ATTN_Q_ROWS = 512
ATTN_KEY_CHUNK = 256
ATTN_HEADS_INTERLEAVED = 1
WIN_ROWS = 128
WIN_BLOCKS_PER_STEP = 16
MERGE_ROWS = 512
MERGE_COL_TILE = 256
DISPATCH_ROWS = 1024
EXPERT_ROWS = 512
EXPERT_COL_TILE = 256
COMBINE_ROWS = 512
DMA_UNROLL = 8

PA_W = A_HEADS * HEAD_DIM + A_KV_HEADS * HEAD_DIM
PA_V = A_KV_HEADS * LANES
PB_W = B_HEADS * HEAD_DIM + B_KV_HEADS * HEAD_DIM + B_KV_HEADS * LANES
PC_W = C_Q_LORA + C_KV_LORA + LANES
OFF_A, OFF_AV = 0, PA_W
OFF_B = OFF_AV + PA_V
OFF_C = OFF_B + PB_W
P1_COLS = OFF_C + PC_W


def _cparams(semantics):
    return pltpu.CompilerParams(dimension_semantics=semantics, vmem_limit_bytes=VMEM_LIMIT_BYTES)


def _layer_norm(x, g, b):
    mu = jnp.mean(x, axis=-1, keepdims=True)
    xc = x - mu
    var = jnp.mean(xc * xc, axis=-1, keepdims=True)
    return xc * lax.rsqrt(var + NORM_EPS) * g + b


def _swap16(x):
    lane = lax.broadcasted_iota(jnp.int32, x.shape, x.ndim - 1)
    up = pltpu.roll(x, LANES - 16, axis=x.ndim - 1)
    down = pltpu.roll(x, 16, axis=x.ndim - 1)
    return jnp.where((lane & 16) == 0, up, down)


def _rope_tiles(x, cos, sin):
    tiles = []
    for c in range(x.shape[-1] // LANES):
        xc = x[:, c * LANES:(c + 1) * LANES]
        tiles.append(xc * cos + _swap16(xc) * sin)
    return jnp.concatenate(tiles, axis=-1) if len(tiles) > 1 else tiles[0]


def _store_vt(vt_ref, v_ext, kv_heads, keep=VT_ROWS):
    vt = v_ext.T
    for kv in range(kv_heads):
        vt_ref[kv * keep:(kv + 1) * keep, :] = vt[kv * LANES:kv * LANES + keep].astype(BF16)


def _proj_kernel(x_ref, lng_ref, lnb_ref, w1_ref, bd_ref, ga_ref, cosa_ref, sina_ref,
                 gcq_ref, gckv_ref, wuq_ref, wukv_ref, cosq_ref, sinq_ref, cosk_ref, sink_ref,
                 vones_a_ref, vones_c_ref,
                 *out_refs, apply_ln):
    if apply_ln:
        xn_ref, qa_ref, ka_ref, va_ref, qb_ref, kb_ref, vb_ref, qc_ref, kc_ref, vc_ref = out_refs
    else:
        qa_ref, ka_ref, va_ref, qb_ref, kb_ref, vb_ref, qc_ref, kc_ref, vc_ref = out_refs
    x = x_ref[...]
    if apply_ln:
        x = _layer_norm(x, lng_ref[...], lnb_ref[...])
        xn_ref[...] = x
    xb = x.astype(BF16)
    pc = jnp.dot(xb, w1_ref[:, OFF_C:OFF_C + PC_W], preferred_element_type=F32)
    pa = jnp.dot(xb, w1_ref[:, OFF_A:OFF_B], preferred_element_type=F32)

    cq = pc[:, :C_Q_LORA]
    ckv = pc[:, C_Q_LORA:C_Q_LORA + C_KV_LORA]
    kr = pc[:, C_Q_LORA + C_KV_LORA:]
    cqn = cq * lax.rsqrt(jnp.mean(cq * cq, axis=-1, keepdims=True) + NORM_EPS) * gcq_ref[...]
    ckvn = ckv * lax.rsqrt(jnp.mean(ckv * ckv, axis=-1, keepdims=True) + NORM_EPS) * gckv_ref[...]
    qc = jnp.dot(cqn.astype(BF16), wuq_ref[...], preferred_element_type=F32)
    kv = jnp.dot(ckvn.astype(BF16), wukv_ref[...], preferred_element_type=F32)
    qk = pa[:, :PA_W]
    ms = jnp.dot((qk * qk).astype(BF16), bd_ref[...], preferred_element_type=F32)
    pb = jnp.dot(xb, w1_ref[:, OFF_B:OFF_C], preferred_element_type=F32)

    qc_ref[...] = _rope_tiles(qc, cosq_ref[...], sinq_ref[...]).astype(BF16)
    kpe = _rope_tiles(kr, cosk_ref[...], sink_ref[...])
    hw = C_HEADS * LANES
    kc_ref[...] = (kv[:, :hw] + jnp.concatenate([kpe] * C_HEADS, axis=-1)).astype(BF16)
    _store_vt(vc_ref, kv[:, hw:] + vones_c_ref[...], C_HEADS)

    qk = qk * lax.rsqrt(ms + NORM_EPS) * ga_ref[...]
    qk = _rope_tiles(qk, cosa_ref[...], sina_ref[...])
    nq = A_HEADS * HEAD_DIM
    qa_ref[...] = qk[:, :nq].astype(BF16)
    kk = qk[:, nq:]
    low = lax.broadcasted_iota(jnp.int32, kk.shape, 1) < HEAD_DIM
    ka_ref[...] = jnp.concatenate([jnp.where(low, kk, 0.0),
                                   jnp.where(low, pltpu.roll(kk, HEAD_DIM, axis=1), 0.0)],
                                  axis=-1).astype(BF16)
    _store_vt(va_ref, pa[:, PA_W:] + vones_a_ref[...], A_KV_HEADS)

    nqb = B_HEADS * HEAD_DIM
    nkb = B_KV_HEADS * HEAD_DIM
    qb_ref[...] = pb[:, :nqb].astype(BF16)
    kb_ref[...] = pb[:, nqb:nqb + nkb].astype(BF16)
    _store_vt(vb_ref, pb[:, nqb + nkb:], B_KV_HEADS, keep=HEAD_DIM)


def _proj_call(x, lng, lnb, w1, bd, ga, cosa, sina, gcq, gckv, wuq, wukv, cosq, sinq, cosk, sink,
               vones_a, vones_c, *, seq, apply_ln):
    n = x.shape[0]
    tm = PROJ_ROWS
    nsb = seq // tm
    row = lambda w: pl.BlockSpec((tm, w), lambda i: (i, 0))
    full = lambda a: pl.BlockSpec(a.shape, lambda i: (0,) * a.ndim, pipeline_mode=pl.Buffered(1))
    tab = pl.BlockSpec((tm, LANES), lambda i: (i % nsb, 0))
    out_w = [A_HEADS * HEAD_DIM, A_KV_HEADS * LANES, -A_KV_HEADS * VT_ROWS,
             B_HEADS * HEAD_DIM, B_KV_HEADS * HEAD_DIM, -B_KV_HEADS * HEAD_DIM,
             C_HEADS * LANES, C_HEADS * LANES, -C_HEADS * VT_ROWS]
    out_shape = [jax.ShapeDtypeStruct((n, w) if w > 0 else (-w, n), BF16) for w in out_w]
    out_specs = [row(w) if w > 0 else pl.BlockSpec((-w, tm), lambda i: (0, i)) for w in out_w]
    if apply_ln:
        out_shape = [jax.ShapeDtypeStruct((n, D_MODEL), F32)] + out_shape
        out_specs = [row(D_MODEL)] + out_specs
    return pl.pallas_call(
        functools.partial(_proj_kernel, apply_ln=apply_ln),
        out_shape=out_shape,
        grid=(n // tm,),
        in_specs=[row(D_MODEL), full(lng), full(lnb), full(w1), full(bd), full(ga), tab, tab,
                  full(gcq), full(gckv), full(wuq), full(wukv), tab, tab, tab, tab,
                  full(vones_a), full(vones_c)],
        out_specs=out_specs,
        compiler_params=_cparams(("parallel",)),
        name="proj_ln" if apply_ln else "proj",
    )(x, lng, lnb, w1, bd, ga, cosa, sina, gcq, gckv, wuq, wukv, cosq, sinq, cosk, sink,
      vones_a, vones_c)


def _dense_attn_kernel(q_ref, k_ref, vt_ref, o_ref, *, heads, kv_heads, dq):
    group = heads // kv_heads
    seq = k_ref.shape[0]
    tq = q_ref.shape[0]
    n_chunks = seq // ATTN_KEY_CHUNK
    units = [(h0 + dh, c) for h0 in range(0, heads, ATTN_HEADS_INTERLEAVED) for c in range(n_chunks)
             for dh in range(ATTN_HEADS_INTERLEAVED)]

    def scores_t(h, c):
        kv = h // group
        qh = q_ref[:, h * dq:(h + 1) * dq]
        kc = k_ref[c * ATTN_KEY_CHUNK:(c + 1) * ATTN_KEY_CHUNK, kv * LANES:(kv + 1) * LANES]
        if dq < LANES:
            kc = kc[:, :dq]
        return lax.dot_general(kc, qh, (((1,), (1,)), ((), ())), preferred_element_type=F32)

    ahead = 2
    st = {u: scores_t(*u) for u in units[:ahead]}
    outs = [None] * heads
    m, acc = {}, {}
    for idx, (h, c) in enumerate(units):
        kv = h // group
        s = st.pop((h, c))
        if idx + ahead < len(units):
            st[units[idx + ahead]] = scores_t(*units[idx + ahead])
        cmax = jnp.max(s, axis=0, keepdims=True)
        m_new = cmax if c == 0 else jnp.maximum(m[h], cmax)
        pt = jnp.exp2(s - m_new).astype(BF16)
        vtc = vt_ref[kv * VT_ROWS:(kv + 1) * VT_ROWS, c * ATTN_KEY_CHUNK:(c + 1) * ATTN_KEY_CHUNK]
        pv = jnp.dot(vtc, pt, preferred_element_type=F32)
        acc[h] = pv if c == 0 else acc[h] * jnp.exp2(m[h] - m_new) + pv
        m[h] = m_new
        if c == n_chunks - 1:
            outs[h] = acc[h][:HEAD_DIM] / acc[h][HEAD_DIM:HEAD_DIM + 1]
    o_ref[...] = jnp.concatenate(outs, axis=0).T.astype(o_ref.dtype)


def _dense_attn_call(q, k, vt, *, batch, seq, heads, kv_heads, dq, name):
    n = q.shape[0]
    tq = ATTN_Q_ROWS
    nq = seq // tq
    return pl.pallas_call(
        functools.partial(_dense_attn_kernel, heads=heads, kv_heads=kv_heads, dq=dq),
        out_shape=jax.ShapeDtypeStruct((n, heads * HEAD_DIM), BF16),
        grid=(batch, nq),
        in_specs=[pl.BlockSpec((tq, heads * dq), lambda b, i: (b * nq + i, 0)),
                  pl.BlockSpec((seq, kv_heads * LANES), lambda b, i: (b, 0)),
                  pl.BlockSpec((kv_heads * VT_ROWS, seq), lambda b, i: (0, b))],
        out_specs=pl.BlockSpec((tq, heads * HEAD_DIM), lambda b, i: (b * nq + i, 0)),
        compiler_params=_cparams(("parallel", "parallel")),
        name=name,
    )(q, k, vt)


def _win_attn_kernel(q_ref, k_ref, vt_ref, bias_ref, o_ref, *, heads, kv_heads, nb):
    group = heads // kv_heads
    tq = WIN_ROWS
    span = 3 * tq
    units = [(j, kv) for j in range(WIN_BLOCKS_PER_STEP) for kv in range(kv_heads)]

    def logits_t(j, kv):
        i = pl.program_id(1) * WIN_BLOCKS_PER_STEP + j
        qs = jnp.concatenate(
            [q_ref[j * tq:(j + 1) * tq, (kv * group + g) * HEAD_DIM:(kv * group + g + 1) * HEAD_DIM]
             for g in range(group)], axis=0)
        bias = bias_ref[kv]
        logit_chunks, vt_chunks = [], []
        for c in range(3):
            blk = i + (c - 1)
            start = pl.multiple_of(jnp.clip(blk, 0, nb - 1) * tq, tq)
            kc = k_ref[pl.ds(start, tq), kv * HEAD_DIM:(kv + 1) * HEAD_DIM]
            vt_chunks.append(vt_ref[kv * HEAD_DIM:(kv + 1) * HEAD_DIM, pl.ds(start, tq)])
            sc = lax.dot_general(kc, qs, (((1,), (1,)), ((), ())), preferred_element_type=F32)
            sc = sc + bias[c * tq:(c + 1) * tq]
            if c != 1:
                sc = jnp.where(jnp.logical_and(blk >= 0, blk <= nb - 1), sc, NEG_INF)
            logit_chunks.append(sc)
        logit_chunks.append(bias[span:])
        return jnp.concatenate(logit_chunks, axis=0), jnp.concatenate(vt_chunks, axis=1)

    ahead = 2
    pending = {u: logits_t(*u) for u in units[:ahead]}
    outs = []
    for idx, (j, kv) in enumerate(units):
        logits, vt = pending.pop((j, kv))
        if idx + ahead < len(units):
            pending[units[idx + ahead]] = logits_t(*units[idx + ahead])
        m = jnp.max(logits, axis=0, keepdims=True)
        pr = jnp.exp2(logits - m)
        denom = jnp.sum(pr, axis=0, keepdims=True)
        ot = jnp.dot(vt, pr[:span].astype(BF16), preferred_element_type=F32) / denom
        outs += [ot[:, g * tq:(g + 1) * tq] for g in range(group)]
        if kv == kv_heads - 1:
            o_ref[j * tq:(j + 1) * tq, :] = jnp.concatenate(outs, axis=0).T.astype(o_ref.dtype)
            outs = []


def _win_attn_call(q, k, vt, bias, *, batch, seq, heads, kv_heads):
    n = q.shape[0]
    ts = WIN_ROWS * WIN_BLOCKS_PER_STEP
    nq = seq // ts
    return pl.pallas_call(
        functools.partial(_win_attn_kernel, heads=heads, kv_heads=kv_heads, nb=seq // WIN_ROWS),
        out_shape=jax.ShapeDtypeStruct((n, heads * HEAD_DIM), BF16),
        grid=(batch, nq),
        in_specs=[pl.BlockSpec((ts, heads * HEAD_DIM), lambda b, i: (b * nq + i, 0)),
                  pl.BlockSpec((seq, kv_heads * HEAD_DIM), lambda b, i: (b, 0)),
                  pl.BlockSpec((kv_heads * HEAD_DIM, seq), lambda b, i: (0, b)),
                  pl.BlockSpec(bias.shape, lambda b, i: (0, 0, 0))],
        out_specs=pl.BlockSpec((ts, heads * HEAD_DIM), lambda b, i: (b * nq + i, 0)),
        compiler_params=_cparams(("parallel", "parallel")),
        name="win_attn",
    )(q, k, vt, bias)


def _store_row_tiles(ref, x):
    rows = x.shape[0]
    for c in range(ROW_TILES):
        ref[pl.ds(c, rows, stride=ROW_TILES), :] = x[:, c * LANES:(c + 1) * LANES]


def _load_row_tiles(ref):
    rows = ref.shape[0] // ROW_TILES
    return jnp.concatenate([ref[pl.ds(c, rows, stride=ROW_TILES), :] for c in range(ROW_TILES)], axis=-1)


def _merge_kernel(x_ref, oa_ref, ob_ref, oc_ref, wg_ref, wa_ref, wb_ref, wc_ref, wo_ref,
                  g_ref, b_ref, rwt_ref, rb_ref, tri_ref,
                  x1_ref, e_ref, w_ref, r_ref, cnt_ref, carry_ref, *, alpha):
    step = pl.program_id(0)

    @pl.when(step == 0)
    def _():
        carry_ref[...] = jnp.zeros_like(carry_ref)

    x = x_ref[...]
    xb = x.astype(BF16)
    d = D_MODEL
    mixers = ((oa_ref[...], wa_ref), (ob_ref[...], wb_ref), (oc_ref[...], wc_ref))
    tiles = []
    for c0 in range(0, d, MERGE_COL_TILE):
        acc = None
        for idx, (o, wbr_ref) in enumerate(mixers):
            gate = jax.nn.sigmoid(jnp.dot(xb, wg_ref[:, idx * d + c0:idx * d + c0 + MERGE_COL_TILE],
                                          preferred_element_type=F32))
            br = jnp.dot(o, wbr_ref[:, c0:c0 + MERGE_COL_TILE], preferred_element_type=F32)
            acc = gate * br if acc is None else acc + gate * br
        tiles.append(acc.astype(BF16))
    merged = jnp.concatenate(tiles, axis=-1)
    mix = jnp.dot(merged, wo_ref[...], preferred_element_type=F32)
    x1 = _layer_norm(alpha * x + mix, g_ref[...], b_ref[...])
    _store_row_tiles(x1_ref, x1)

    logits = lax.dot_general(rwt_ref[...], x1, (((1,), (1,)), ((), ())),
                             preferred_element_type=F32, precision=lax.Precision.HIGHEST)
    scores = jax.nn.sigmoid(logits)
    biased = scores + rb_ref[...]
    shape = biased.shape
    rows = lax.broadcasted_iota(jnp.int32, shape, 0)
    sub = lax.broadcasted_iota(jnp.int32, (EXPERTS_PER_GROUP, shape[1]), 0)
    best, sel = None, None
    for g in range(N_GROUPS):
        blk = biased[g * EXPERTS_PER_GROUP:(g + 1) * EXPERTS_PER_GROUP]
        m1 = jnp.max(blk, axis=0, keepdims=True)
        i1 = jnp.min(jnp.where(blk == m1, sub, EXPERTS_PER_GROUP), axis=0, keepdims=True)
        m2 = jnp.max(jnp.where(sub == i1, -jnp.inf, blk), axis=0, keepdims=True)
        gs = m1 + m2
        if best is None:
            best, sel = gs, jnp.zeros_like(i1)
        else:
            better = gs > best
            sel = jnp.where(better, g, sel)
            best = jnp.where(better, gs, best)
    in_group = (rows >> 2) == sel
    bm = jnp.where(in_group, biased, -jnp.inf)
    t1 = jnp.max(bm, axis=0, keepdims=True)
    e0 = jnp.min(jnp.where(bm == t1, rows, N_EXPERTS), axis=0, keepdims=True)
    bm2 = jnp.where(rows == e0, -jnp.inf, bm)
    t2 = jnp.max(bm2, axis=0, keepdims=True)
    e1 = jnp.min(jnp.where(bm2 == t2, rows, N_EXPERTS), axis=0, keepdims=True)
    hot0 = rows == e0
    hot1 = rows == e1
    s0 = jnp.sum(jnp.where(hot0, scores, 0.0), axis=0, keepdims=True)
    s1 = jnp.sum(jnp.where(hot1, scores, 0.0), axis=0, keepdims=True)
    tot = s0 + s1
    e_ref[...] = jnp.concatenate([e0, e1], axis=0)
    w_ref[...] = jnp.concatenate([s0 / tot, s1 / tot], axis=0)

    hot = jnp.where(jnp.logical_or(hot0, hot1), 1.0, 0.0)
    prefix = jnp.dot(hot.astype(BF16), tri_ref[...], preferred_element_type=F32) + carry_ref[:, 0:1]
    r0 = jnp.sum(jnp.where(hot0, prefix, 0.0), axis=0, keepdims=True)
    r1 = jnp.sum(jnp.where(hot1, prefix, 0.0), axis=0, keepdims=True)
    r_ref[...] = jnp.concatenate([r0, r1], axis=0).astype(jnp.int32)
    carry_ref[...] = carry_ref[...] + jnp.sum(hot, axis=1, keepdims=True)
    cnt_ref[...] = carry_ref[...]


def _merge_call(x, oa, ob, oc, wg, wa, wb, wc, wo, g, b, rwt, rb, tri, *, alpha):
    n = x.shape[0]
    tm = MERGE_ROWS
    row = lambda w: pl.BlockSpec((tm, w), lambda i: (i, 0))
    full = lambda a: pl.BlockSpec(a.shape, lambda i: (0,) * a.ndim, pipeline_mode=pl.Buffered(1))
    tok = pl.BlockSpec((TOP_K, tm), lambda i: (0, i))
    slabs = pl.BlockSpec((tm * ROW_TILES, LANES), lambda i: (i, 0))
    return pl.pallas_call(
        functools.partial(_merge_kernel, alpha=alpha),
        out_shape=[jax.ShapeDtypeStruct((n * ROW_TILES, LANES), F32),
                   jax.ShapeDtypeStruct((TOP_K, n), jnp.int32),
                   jax.ShapeDtypeStruct((TOP_K, n), F32),
                   jax.ShapeDtypeStruct((TOP_K, n), jnp.int32),
                   jax.ShapeDtypeStruct((N_EXPERTS, LANES), F32)],
        grid=(n // tm,),
        in_specs=[row(D_MODEL), row(oa.shape[1]), row(ob.shape[1]), row(oc.shape[1]),
                  full(wg), full(wa), full(wb), full(wc), full(wo), full(g), full(b),
                  full(rwt), full(rb), full(tri)],
        out_specs=[slabs, tok, tok, tok,
                   pl.BlockSpec((N_EXPERTS, LANES), lambda i: (0, 0))],
        scratch_shapes=[pltpu.VMEM((N_EXPERTS, LANES), F32)],
        compiler_params=_cparams(("arbitrary",)),
        name="merge",
    )(x, oa, ob, oc, wg, wa, wb, wc, wo, g, b, rwt, rb, tri)


def _row_slab(ref, first):
    if not isinstance(first, int):
        first = pl.multiple_of(first, ROW_TILES)
    return ref.at[pl.ds(first, ROW_TILES)]


def _dispatch_kernel(pad_first_ref, pad_rows_ref, tail_ref, pos_ref, x_ref, dst_hbm, zero_ref, sem, zero_sem):
    tb = DISPATCH_ROWS

    @pl.when(pl.program_id(0) == 0)
    def _():
        zero_ref[...] = jnp.zeros_like(zero_ref)

        def for_each_fill(act):
            for e in range(N_EXPERTS):
                first, n = pad_first_ref[e], pad_rows_ref[e]
                for bit in reversed(range(EXPERT_ROWS.bit_length() - 1)):
                    run = (1 << bit) * ROW_TILES
                    has = (n & (1 << bit)) != 0

                    @pl.when(has)
                    def _(first=first, run=run):
                        act(pltpu.make_async_copy(zero_ref.at[pl.ds(0, run)],
                                                  dst_hbm.at[pl.ds(pl.multiple_of(first, ROW_TILES), run)],
                                                  zero_sem))
                    first = first + jnp.where(has, run, 0)
            tile = EXPERT_ROWS * ROW_TILES
            for k in range(N_EXPERTS):
                @pl.when(k < tail_ref[1])
                def _(k=k):
                    act(pltpu.make_async_copy(
                        zero_ref, dst_hbm.at[pl.ds(pl.multiple_of((tail_ref[0] + k) * tile, tile), tile)], zero_sem))

        for_each_fill(lambda copy: copy.start())
        for_each_fill(lambda copy: copy.wait())

    def start(g, carry):
        for u in range(DMA_UNROLL):
            t = g * DMA_UNROLL + u
            for j in range(TOP_K):
                pltpu.make_async_copy(_row_slab(x_ref, t * ROW_TILES),
                                      _row_slab(dst_hbm, pos_ref[0, 0, j * tb + t]),
                                      sem).start(priority=(u + j) % 2)
        return carry

    lax.fori_loop(0, tb // DMA_UNROLL, start, 0)
    for j in range(TOP_K):
        pltpu.make_async_copy(x_ref, dst_hbm.at[pl.ds(0, tb * ROW_TILES)], sem).wait()


def _dispatch_call(pad_first, pad_rows, tail, pos_flat, x1, *, p_rows):
    nsteps = pos_flat.shape[0]
    tb = DISPATCH_ROWS
    grid_spec = pltpu.PrefetchScalarGridSpec(
        num_scalar_prefetch=3,
        grid=(nsteps,),
        in_specs=[pl.BlockSpec((1, 1, TOP_K * tb), lambda i, *_: (i, 0, 0), memory_space=pltpu.SMEM),
                  pl.BlockSpec((tb * ROW_TILES, LANES), lambda i, *_: (i, 0))],
        out_specs=pl.BlockSpec(memory_space=pl.ANY),
        scratch_shapes=[pltpu.VMEM((EXPERT_ROWS * ROW_TILES, LANES), F32),
                        pltpu.SemaphoreType.DMA(()), pltpu.SemaphoreType.DMA(())],
    )
    return pl.pallas_call(
        _dispatch_kernel,
        out_shape=jax.ShapeDtypeStruct((p_rows * ROW_TILES, LANES), F32),
        grid_spec=grid_spec,
        compiler_params=pltpu.CompilerParams(dimension_semantics=("arbitrary",), has_side_effects=True,
                                             vmem_limit_bytes=VMEM_LIMIT_BYTES),
        name="dispatch",
    )(pad_first, pad_rows, tail, pos_flat, x1)


def _expert_kernel(te_ref, nu_ref, xs_ref, wg_ref, wu_ref, wd_ref, ys_ref, wgu_bf, wd_bf):
    i = pl.program_id(0)

    @pl.when(jnp.logical_or(i == 0, te_ref[i] != te_ref[jnp.maximum(i - 1, 0)]))
    def _():
        wgu_bf[:, :D_FF] = wg_ref[0, 0].astype(BF16)
        wgu_bf[:, D_FF:] = wu_ref[0, 0].astype(BF16)
        wd_bf[...] = wd_ref[0, 0].astype(BF16)

    @pl.when(i < nu_ref[0])
    def _():
        xb = _load_row_tiles(xs_ref).astype(BF16)
        gu = [(jnp.dot(xb, wgu_bf[:, c0:c0 + EXPERT_COL_TILE], preferred_element_type=F32),
               jnp.dot(xb, wgu_bf[:, D_FF + c0:D_FF + c0 + EXPERT_COL_TILE], preferred_element_type=F32))
              for c0 in range(0, D_FF, EXPERT_COL_TILE)]
        hid = [((gate * jax.nn.sigmoid(gate)) * up).astype(BF16) for gate, up in gu]
        y = jnp.dot(jnp.concatenate(hid, axis=-1), wd_bf[...], preferred_element_type=F32)
        _store_row_tiles(ys_ref, y)

    @pl.when(i >= nu_ref[0])
    def _():
        ys_ref[...] = jnp.zeros_like(ys_ref)


def _expert_call(tile_expert, n_used, xs, w_gate, w_up, w_down, *, layer):
    p = xs.shape[0] // ROW_TILES
    tm = EXPERT_ROWS
    wspec = lambda r, c: pl.BlockSpec((1, 1, r, c), lambda i, te, nu: (layer, te[i], 0, 0))
    grid_spec = pltpu.PrefetchScalarGridSpec(
        num_scalar_prefetch=2,
        grid=(p // tm,),
        in_specs=[pl.BlockSpec((tm * ROW_TILES, LANES), lambda i, te, nu: (i, 0)),
                  wspec(D_MODEL, D_FF), wspec(D_MODEL, D_FF), wspec(D_FF, D_MODEL)],
        out_specs=pl.BlockSpec((tm * ROW_TILES, LANES), lambda i, te, nu: (i, 0)),
        scratch_shapes=[pltpu.VMEM((D_MODEL, 2 * D_FF), BF16), pltpu.VMEM((D_FF, D_MODEL), BF16)],
    )
    return pl.pallas_call(
        _expert_kernel,
        out_shape=jax.ShapeDtypeStruct((p * ROW_TILES, LANES), F32),
        grid_spec=grid_spec,
        compiler_params=_cparams(("arbitrary",)),
        name="experts",
    )(tile_expert, n_used, xs, w_gate, w_up, w_down)


def _combine_kernel(pos_ref, nxt_ref, ys_hbm, x_ref, w_ref, g_ref, b_ref, o_ref, buf_ref, sem, *, alpha):
    tc = COMBINE_ROWS
    i = pl.program_id(0)
    slot = i % 2

    def row_copy(idx_ref, to_slot, t, j):
        return pltpu.make_async_copy(_row_slab(ys_hbm, idx_ref[0, 0, j * tc + t]),
                                     _row_slab(buf_ref.at[to_slot, j], t * ROW_TILES), sem.at[to_slot])

    def wait_slot(s):
        for j in range(TOP_K):
            pltpu.make_async_copy(ys_hbm.at[pl.ds(0, tc * ROW_TILES)], buf_ref.at[s, j], sem.at[s]).wait()

    @pl.when(i == 0)
    def _():
        def start(g, carry):
            for u in range(DMA_UNROLL):
                for j in range(TOP_K):
                    row_copy(pos_ref, 0, g * DMA_UNROLL + u, j).start(priority=(u + j) % 2)
            return carry
        lax.fori_loop(0, tc // DMA_UNROLL, start, 0)

    for t in range(tc):
        for j in range(TOP_K):
            row_copy(nxt_ref, 1 - slot, t, j).start(priority=(t + j) % 2)
    wait_slot(slot)
    w = w_ref[...]
    ffn = (w[:, 0:1] * _load_row_tiles(buf_ref.at[slot, 0])
           + w[:, 1:2] * _load_row_tiles(buf_ref.at[slot, 1]))
    o_ref[...] = _layer_norm(alpha * _load_row_tiles(x_ref) + ffn, g_ref[...], b_ref[...])

    @pl.when(i == pl.num_programs(0) - 1)
    def _():
        wait_slot(1 - slot)


def _combine_call(pos_flat, ys, x1, wts_t, g, b, *, alpha):
    n = x1.shape[0] // ROW_TILES
    tc = COMBINE_ROWS
    nsteps = n // tc
    idx = lambda shift: pl.BlockSpec((1, 1, TOP_K * tc), lambda i: (jnp.minimum(i + shift, nsteps - 1), 0, 0),
                                     memory_space=pltpu.SMEM)
    return pl.pallas_call(
        functools.partial(_combine_kernel, alpha=alpha),
        out_shape=jax.ShapeDtypeStruct((n, D_MODEL), F32),
        grid=(nsteps,),
        in_specs=[idx(0), idx(1),
                  pl.BlockSpec(memory_space=pl.ANY),
                  pl.BlockSpec((tc * ROW_TILES, LANES), lambda i: (i, 0)),
                  pl.BlockSpec((tc, TOP_K), lambda i: (i, 0)),
                  pl.BlockSpec(g.shape, lambda i: (0, 0)),
                  pl.BlockSpec(b.shape, lambda i: (0, 0))],
        out_specs=pl.BlockSpec((tc, D_MODEL), lambda i: (i, 0)),
        scratch_shapes=[pltpu.VMEM((2, TOP_K, tc * ROW_TILES, LANES), F32),
                        pltpu.SemaphoreType.DMA((2,))],
        compiler_params=_cparams(("arbitrary",)),
        name="combine",
    )(pos_flat, pos_flat, ys, x1, wts_t, g, b)


def _t5_bucket(rel):
    nb = N_BUCKETS // 2
    ret = (rel > 0).astype(np.int32) * nb
    n = np.abs(rel)
    max_exact = nb // 2
    large = max_exact + (np.log(np.maximum(n, 1) / max_exact)
                         / math.log(MAX_DISTANCE / max_exact) * (nb - max_exact)).astype(np.int32)
    large = np.minimum(large, nb - 1)
    return (ret + np.where(n < max_exact, n, large)).astype(np.int32)


def _rope_tables(seq):
    def cs(pos, dim):
        inv = ROPE_THETA ** (-(jnp.arange(0, dim, 2, dtype=F32) / dim))
        ang = pos.astype(F32)[:, None] * inv[None, :]
        return jnp.cos(ang), jnp.sin(ang)

    t = jnp.arange(seq)
    cr, sr = cs(t // GRID_W, HEAD_DIM // 2)
    cc, sc = cs(t % GRID_W, HEAD_DIM // 2)
    cos_head = jnp.concatenate([cr, cr, cc, cc], axis=-1)
    sin_head = jnp.concatenate([-sr, sr, -sc, sc], axis=-1)
    cos_a = jnp.concatenate([cos_head, cos_head], axis=-1)
    sin_a = jnp.concatenate([sin_head, sin_head], axis=-1)
    cs_, ss_ = cs(t, C_ROPE)
    ones = jnp.ones((seq, C_NOPE), F32)
    pad1 = jnp.ones((seq, LANES - C_NOPE - C_ROPE), F32)
    cos_k = jnp.concatenate([ones, cs_, cs_, pad1], axis=-1)
    sin_k = jnp.concatenate([0 * ones, -ss_, ss_, 0 * pad1], axis=-1)
    scale = (C_NOPE + C_ROPE) ** -0.5 * LOG2E
    return cos_a, sin_a, cos_k * scale, sin_k * scale, cos_k, sin_k


def _window_bias(rpb_table, b_sink_l):
    span = WIN_ROWS + 2 * WINDOW
    rel = np.arange(span)[None, :] - WINDOW - np.arange(WIN_ROWS)[:, None]
    band = jnp.asarray(np.abs(rel) <= WINDOW)
    onehot = jnp.asarray(_t5_bucket(rel)[:, :, None] == np.arange(N_BUCKETS), F32)
    bias = jnp.einsum("qsb,bh->qsh", onehot, rpb_table.astype(F32),
                      precision=lax.Precision.HIGHEST)
    bias = jnp.where(band[:, :, None], bias * LOG2E, NEG_INF)
    bias = jnp.transpose(bias, (1, 2, 0))
    sink = jnp.full((8, B_HEADS, WIN_ROWS), NEG_INF, F32)
    sink = sink.at[0].set(jnp.broadcast_to(b_sink_l.astype(F32)[:, None] * LOG2E, (B_HEADS, WIN_ROWS)))
    ext = jnp.concatenate([bias, sink], axis=0)
    group = B_HEADS // B_KV_HEADS
    ext = ext.reshape(span + 8, B_KV_HEADS, group * WIN_ROWS)
    return jnp.transpose(ext, (1, 0, 2))


def _pad_cols(w, groups, width, total):
    k = w.shape[0]
    w = w.reshape(k, groups, width)
    w = jnp.pad(w, ((0, 0), (0, 0), (0, total - width)))
    return w.reshape(k, groups * total)


def _layer_params(l, w_in, a_q_norm_g, a_k_norm_g, c_q_norm_g, c_kv_norm_g, c_w_uq, c_w_ukv,
                  w_branch_a, w_branch_b, w_branch_c, w_o):
    w = w_in[l]
    splits = np.cumsum([0, A_HEADS * HEAD_DIM, A_KV_HEADS * HEAD_DIM, A_KV_HEADS * HEAD_DIM,
                        B_HEADS * HEAD_DIM, B_KV_HEADS * HEAD_DIM, B_KV_HEADS * HEAD_DIM,
                        C_Q_LORA, C_KV_LORA, C_ROPE, D_MODEL, D_MODEL, D_MODEL])
    part = [w[:, splits[i]:splits[i + 1]] for i in range(12)]
    qa, ka, va, qb, kb, vb, cq, ckv, kr, ga, gb, gc = part
    va_e = _pad_cols(va, A_KV_HEADS, HEAD_DIM, LANES)
    kr_p = jnp.pad(kr, ((0, 0), (C_NOPE, LANES - C_NOPE - C_ROPE)))
    vb_e = _pad_cols(vb, B_KV_HEADS, HEAD_DIM, LANES)
    w1 = jnp.concatenate([qa, ka, va_e, qb * (HEAD_DIM ** -0.5 * LOG2E), kb, vb_e, cq, ckv, kr_p],
                         axis=1).astype(BF16)
    gate_a = jnp.concatenate([jnp.tile(a_q_norm_g[l], A_HEADS) * (HEAD_DIM ** -0.5 * LOG2E),
                              jnp.tile(a_k_norm_g[l], A_KV_HEADS)])[None, :].astype(F32)
    wuq = _pad_cols(c_w_uq[l], C_HEADS, C_NOPE + C_ROPE, LANES).astype(BF16)
    ukv = c_w_ukv[l].reshape(C_KV_LORA, C_HEADS, C_NOPE + C_V)
    wuk = jnp.pad(ukv[:, :, :C_NOPE], ((0, 0), (0, 0), (0, LANES - C_NOPE))).reshape(C_KV_LORA, C_HEADS * LANES)
    wuv = jnp.pad(ukv[:, :, C_NOPE:], ((0, 0), (0, 0), (0, LANES - C_V))).reshape(C_KV_LORA, C_HEADS * LANES)
    wukv = jnp.concatenate([wuk, wuv], axis=1).astype(BF16)
    wg = jnp.concatenate([ga, gb, gc], axis=1).astype(BF16)
    return dict(w1=w1, gate_a=gate_a, gcq=c_q_norm_g[l][None, :].astype(F32),
                gckv=c_kv_norm_g[l][None, :].astype(F32), wuq=wuq, wukv=wukv, wg=wg,
                wa=w_branch_a[l].astype(BF16), wb=w_branch_b[l].astype(BF16),
                wc=w_branch_c[l].astype(BF16), wo=w_o[l].astype(BF16))


def kernel(x, ln_in_g, ln_in_b, w_in, a_q_norm_g, a_k_norm_g, b_sink, rpb_table, c_q_norm_g, c_kv_norm_g,
           c_w_uq, c_w_ukv, w_branch_a, w_branch_b, w_branch_c, w_o, ln1_g, ln1_b, router_w, router_bias,
           w_gate, w_up, w_down, ln2_g, ln2_b):
    batch, seq, d = x.shape
    depth = w_in.shape[0]
    n = batch * seq
    alpha = (2 * depth) ** 0.25
    assert d == D_MODEL and seq % PROJ_ROWS == 0 and seq % ATTN_Q_ROWS == 0
    assert seq % (WIN_ROWS * WIN_BLOCKS_PER_STEP) == 0
    assert n % DISPATCH_ROWS == 0 and n % MERGE_ROWS == 0 and n % COMBINE_ROWS == 0

    cos_a, sin_a, cos_q, sin_q, cos_k, sin_k = _rope_tables(seq)
    seg = np.arange(PA_W) // HEAD_DIM
    bd = jnp.asarray((seg[:, None] == seg[None, :]).astype(np.float32) / HEAD_DIM, BF16)
    ones_pat = np.zeros((1, LANES), np.float32)
    ones_pat[0, HEAD_DIM:] = 1.0
    vones_a = jnp.asarray(np.tile(ones_pat, (1, A_KV_HEADS)))
    vones_c = jnp.asarray(np.tile(ones_pat, (1, C_HEADS)))
    tri = jnp.asarray(np.triu(np.ones((MERGE_ROWS, MERGE_ROWS), np.float32), k=1), BF16)
    rwt = router_w.astype(F32).T
    rb = router_bias.astype(F32)[:, None]
    lng = ln_in_g.astype(F32)[None, :]
    lnb = ln_in_b.astype(F32)[None, :]

    tme = EXPERT_ROWS
    n_tiles = (TOP_K * n) // tme + N_EXPERTS
    p_rows = n_tiles * tme

    h = x.reshape(n, d).astype(F32)
    for l in range(depth):
        prm = _layer_params(l, w_in, a_q_norm_g, a_k_norm_g, c_q_norm_g, c_kv_norm_g, c_w_uq, c_w_ukv,
                            w_branch_a, w_branch_b, w_branch_c, w_o)
        outs = _proj_call(h, lng, lnb, prm["w1"], bd, prm["gate_a"], cos_a, sin_a, prm["gcq"], prm["gckv"],
                          prm["wuq"], prm["wukv"], cos_q, sin_q, cos_k, sin_k, vones_a, vones_c,
                          seq=seq, apply_ln=(l == 0))
        if l == 0:
            h, *outs = outs
        qa, ka, va, qb, kb, vb, qc, kc, vc = outs
        oa = _dense_attn_call(qa, ka, va, batch=batch, seq=seq, heads=A_HEADS, kv_heads=A_KV_HEADS,
                              dq=HEAD_DIM, name="attn_a")
        bias = _window_bias(rpb_table, b_sink[l])
        ob = _win_attn_call(qb, kb, vb, bias, batch=batch, seq=seq, heads=B_HEADS, kv_heads=B_KV_HEADS)
        oc = _dense_attn_call(qc, kc, vc, batch=batch, seq=seq, heads=C_HEADS, kv_heads=C_HEADS,
                              dq=LANES, name="attn_c")
        x1, e_idx, wts, rank, cnt = _merge_call(
            h, oa, ob, oc, prm["wg"], prm["wa"], prm["wb"], prm["wc"], prm["wo"],
            ln1_g[l].astype(F32)[None, :], ln1_b[l].astype(F32)[None, :], rwt, rb, tri, alpha=alpha)

        counts = cnt[:, 0].astype(jnp.int32)
        padded = ((counts + tme - 1) // tme) * tme
        ends = jnp.cumsum(padded)
        offs = ends - padded
        pos = rank
        for e in range(N_EXPERTS):
            pos = pos + jnp.where(e_idx == e, offs[e], 0)
        tile_start = jnp.arange(n_tiles, dtype=jnp.int32) * tme
        tile_expert = jnp.minimum(jnp.sum(tile_start[:, None] >= ends[None, :], axis=1), N_EXPERTS - 1)
        tile_expert = tile_expert.astype(jnp.int32)
        n_used = (ends[-1:] // tme).astype(jnp.int32)

        first_row = pos * ROW_TILES

        def per_step(rows):
            return first_row.reshape(TOP_K, n // rows, rows).transpose(1, 0, 2).reshape(n // rows, 1, TOP_K * rows)

        pad_first = ((offs + counts) * ROW_TILES).astype(jnp.int32)
        pad_rows = (padded - counts).astype(jnp.int32)
        tail = jnp.concatenate([n_used, n_tiles - n_used]).astype(jnp.int32)
        xs = _dispatch_call(pad_first, pad_rows, tail, per_step(DISPATCH_ROWS), x1, p_rows=p_rows)
        ys = _expert_call(tile_expert, n_used, xs, w_gate.astype(F32), w_up.astype(F32), w_down.astype(F32),
                          layer=l)
        pos_c = per_step(COMBINE_ROWS)
        h = _combine_call(pos_c, ys, x1, wts.T, ln2_g[l].astype(F32)[None, :], ln2_b[l].astype(F32)[None, :],
                          alpha=alpha)
    return h.reshape(batch, seq, d).astype(x.dtype)
```
